```python
import math
import jax, jax.numpy as jnp
from jax import lax
import numpy as np

D_MODEL = 1024
BATCH = 4
SEQ = 4096
DEPTH = 1
DEC_BATCH = 32
DEC_SEQ = 8
PAST_LEN = 16384
PAGE_SIZE = 128

HD = 64
NSA_HEADS = 8
NSA_KV_HEADS = 2
NSA_GROUP = NSA_HEADS // NSA_KV_HEADS
NSA_BLOCK = 64
NSA_TOPN = 16
NSA_WINDOW = 512
NSA_KINDS = 6
DIFF_HEADS = 4
DIFF_VD = 2 * HD
D_FF = 2816
Q_BLOCK = 128
EPS = 1e-6
FORCE_SCORE = 1e4
SPLITS = (NSA_HEADS * HD, NSA_KINDS * NSA_KV_HEADS * HD, 3 * NSA_HEADS,
          DIFF_HEADS * 2 * HD, DIFF_HEADS * 2 * HD, DIFF_HEADS * DIFF_VD, 2 * D_MODEL)
IN_WIDTH = sum(SPLITS)

kernel_name = 'nsa_diffattn_macaron_hybrid_step'


def rmsnorm(x, g):
    xf = x.astype(jnp.float32)
    y = xf * lax.rsqrt(jnp.mean(xf * xf, axis=-1, keepdims=True) + EPS)
    return (y * g.astype(jnp.float32)).astype(x.dtype)


def swiglu(x, w1, w3, w2):
    return (jax.nn.silu(x @ w1) * (x @ w3)) @ w2


def alibi_slopes(n):
    return 2.0 ** (-8.0 * jnp.arange(1, n + 1, dtype=jnp.float32) / n)


def masked_softmax(s, mask, axes):
    s = jnp.where(mask, s.astype(jnp.float32), -1e30)
    m = jnp.max(s, axis=axes, keepdims=True)
    p = jnp.where(mask, jnp.exp(s - m), 0.0)
    return p / jnp.maximum(jnp.sum(p, axis=axes, keepdims=True), 1e-30)


def project(h, w_in):
    B_, T = h.shape[:2]
    offs = [int(o) for o in np.cumsum(SPLITS)[:-1]]
    nq, nkv, ng, dq, dk, dv, mg = jnp.split(h @ w_in, offs, axis=-1)
    nq = nq.reshape(B_, T, NSA_KV_HEADS, NSA_GROUP, HD)
    nkv = nkv.reshape(B_, T, NSA_KINDS, NSA_KV_HEADS, HD)
    ng = jax.nn.sigmoid(ng.reshape(B_, T, NSA_KV_HEADS, NSA_GROUP, 3))
    dq = dq.reshape(B_, T, DIFF_HEADS, 2, HD)
    dk = dk.reshape(B_, T, DIFF_HEADS, 2, HD)
    dv = dv.reshape(B_, T, DIFF_HEADS, DIFF_VD)
    mg = jax.nn.sigmoid(mg.reshape(B_, T, 2, D_MODEL))
    return nq, nkv, ng, dq, dk, dv, mg


def nsa_compress(kb, w1, b1, w2):
    hid = jax.nn.gelu(jnp.einsum('bnphd,pde->bnhe', kb, w1) + b1)
    return hid @ w2


def nsa_prepare(kv4, cmp_w1, cmp_b1, cmp_w2):
    B_, L = kv4.shape[:2]
    blocks = kv4.reshape(B_, L // NSA_BLOCK, NSA_BLOCK, 4, NSA_KV_HEADS, HD)
    ck = nsa_compress(blocks[:, :, :, 0], cmp_w1[0], cmp_b1[0], cmp_w2[0])
    cv = nsa_compress(blocks[:, :, :, 1], cmp_w1[1], cmp_b1[1], cmp_w2[1])
    ks_b = jnp.transpose(blocks[:, :, :, 2], (0, 3, 1, 2, 4))
    vs_b = jnp.transpose(blocks[:, :, :, 3], (0, 3, 1, 2, 4))
    return ck, cv, ks_b, vs_b


def nsa_cmp_sel(q, qpos, ck, cv, ks_b, vs_b, slopes):
    B_, nb = ck.shape[0], ck.shape[1]
    scale = HD ** -0.5
    blk = jnp.arange(nb, dtype=jnp.int32)
    dist_c = (qpos[:, None] - ((blk + 1) * NSA_BLOCK - 1)[None, :]).astype(jnp.float32)
    valid_c = dist_c >= 0.0
    s = jnp.einsum('bthgd,bnhd->bthgn', q, ck).astype(jnp.float32) * scale
    s = s - slopes[None, None, :, :, None] * dist_c[None, :, None, None, :]
    p_c = masked_softmax(s, valid_c[None, :, None, None, :], -1)
    o_c = jnp.einsum('bthgn,bnhd->bthgd', p_c.astype(cv.dtype), cv)
    cur = qpos // NSA_BLOCK
    imp = jnp.sum(p_c, axis=3)
    forced = (blk[None, :] == cur[:, None]) | (blk[None, :] == 0)
    allowed = blk[None, :] <= cur[:, None]
    imp = jnp.where(forced[None, :, None, :], FORCE_SCORE,
                    jnp.where(allowed[None, :, None, :], imp, -1.0))
    n_sel = min(NSA_TOPN, nb)
    top_v, top_i = lax.top_k(imp, n_sel)
    sel_ok = top_v >= 0.0
    bi = jnp.arange(B_)[:, None, None, None]
    hi = jnp.arange(NSA_KV_HEADS)[None, None, :, None]
    kg = ks_b[bi, hi, top_i]
    vg = vs_b[bi, hi, top_i]
    kpos = top_i[..., None] * NSA_BLOCK + jnp.arange(NSA_BLOCK, dtype=jnp.int32)
    dist_s = (qpos[None, :, None, None, None] - kpos).astype(jnp.float32)
    valid_s = sel_ok[..., None] & (dist_s >= 0.0)
    s2 = jnp.einsum('bthgd,bthjkd->bthgjk', q, kg).astype(jnp.float32) * scale
    s2 = s2 - slopes[None, None, :, :, None, None] * dist_s[:, :, :, None]
    p_s = masked_softmax(s2, valid_s[:, :, :, None], (-2, -1))
    o_s = jnp.einsum('bthgjk,bthjkd->bthgd', p_s.astype(vg.dtype), vg)
    return o_c, o_s


def window_attn(q, qpos, k, v, kpos, slopes):
    dist = qpos[:, None] - kpos[None, :]
    valid = (dist >= 0) & (dist < NSA_WINDOW) & (kpos[None, :] >= 0)
    s = jnp.einsum('bthgd,bshd->bthgs', q, k).astype(jnp.float32) * (HD ** -0.5)
    s = s - slopes[None, None, :, :, None] * dist.astype(jnp.float32)[None, :, None, None, :]
    p = masked_softmax(s, valid[None, :, None, None, :], -1)
    return jnp.einsum('bthgs,bshd->bthgd', p.astype(v.dtype), v)


def diff_lambda(lqk, lam_init):
    l = lqk.astype(jnp.float32)
    return jnp.exp(jnp.sum(l[0] * l[1])) - jnp.exp(jnp.sum(l[2] * l[3])) + lam_init


def diff_attn(q, k, v, qpos, kpos, lam, slopes):
    dist = qpos[:, None] - kpos[None, :]
    valid = dist >= 0
    s = jnp.einsum('bthcd,bshcd->bhcts', q, k).astype(jnp.float32) * (HD ** -0.5)
    s = s - slopes[None, :, None, None, None] * dist.astype(jnp.float32)[None, None, None]
    p = masked_softmax(s, valid[None, None, None], -1)
    a = p[:, :, 0] - lam * p[:, :, 1]
    return jnp.einsum('bhts,bshe->bthe', a.astype(v.dtype), v)


def gate_nsa(g, o_c, o_s, o_w):
    return g[..., 0:1] * o_c + g[..., 1:2] * o_s + g[..., 2:3] * o_w


def merge(o_nsa, o_diff, mg, P, lam_init):
    B_, T = o_diff.shape[:2]
    od = (rmsnorm(o_diff, P['g_diff_subln']) * (1.0 - lam_init)).reshape(B_, T, DIFF_HEADS * DIFF_VD)
    m = mg[:, :, 0] * (o_nsa @ P['w_up_nsa']) + mg[:, :, 1] * (od @ P['w_up_diff'])
    return m @ P['w_out']


def prompt_mix(h, P, lam_init):
    B_, T = h.shape[:2]
    nq, nkv, ng, dq, dk, dv, mg = project(h, P['w_in'])
    lam = diff_lambda(P['lambda_qk'], lam_init)
    nsl = alibi_slopes(NSA_HEADS).reshape(NSA_KV_HEADS, NSA_GROUP)
    dsl = alibi_slopes(DIFF_HEADS)
    ck, cv, ks_b, vs_b = nsa_prepare(nkv[:, :, :4], P['cmp_w1'], P['cmp_b1'], P['cmp_w2'])
    pad = ((0, 0), (NSA_WINDOW, 0), (0, 0), (0, 0))
    kw_pad = jnp.pad(nkv[:, :, 4], pad)
    vw_pad = jnp.pad(nkv[:, :, 5], pad)
    pos = jnp.arange(T, dtype=jnp.int32)

    def block(i):
        s0 = i * Q_BLOCK
        qpos = s0 + jnp.arange(Q_BLOCK, dtype=jnp.int32)
        qb = lax.dynamic_slice_in_dim(nq, s0, Q_BLOCK, axis=1)
        gb = lax.dynamic_slice_in_dim(ng, s0, Q_BLOCK, axis=1)
        o_c, o_s = nsa_cmp_sel(qb, qpos, ck, cv, ks_b, vs_b, nsl)
        kwb = lax.dynamic_slice_in_dim(kw_pad, s0, NSA_WINDOW + Q_BLOCK, axis=1)
        vwb = lax.dynamic_slice_in_dim(vw_pad, s0, NSA_WINDOW + Q_BLOCK, axis=1)
        kpos = s0 - NSA_WINDOW + jnp.arange(NSA_WINDOW + Q_BLOCK, dtype=jnp.int32)
        o_w = window_attn(qb, qpos, kwb, vwb, kpos, nsl)
        dqb = lax.dynamic_slice_in_dim(dq, s0, Q_BLOCK, axis=1)
        o_d = diff_attn(dqb, dk, dv, qpos, pos, lam, dsl)
        return gate_nsa(gb, o_c, o_s, o_w), o_d

    o_nsa, o_d = lax.map(block, jnp.arange(T // Q_BLOCK, dtype=jnp.int32))
    o_nsa = jnp.moveaxis(o_nsa, 0, 1).reshape(B_, T, NSA_HEADS * HD)
    o_d = jnp.moveaxis(o_d, 0, 1).reshape(B_, T, DIFF_HEADS, DIFF_VD)
    out = merge(o_nsa, o_d, mg, P, lam_init)
    w_rows = min(NSA_WINDOW, T)
    new_nsa = nkv[:, :, :4]
    new_diff = jnp.stack([dk.reshape(B_, T, DIFF_HEADS, 2 * HD), dv], axis=2)
    new_win = nkv[:, T - w_rows:, 4:]
    return out, (new_nsa, new_diff, new_win)


def sample_mix(h, c_nsa, c_diff, win, page_table, P, lam_init):
    Bd, Tn = h.shape[:2]
    past = page_table.shape[1] * c_nsa.shape[1]
    L = past + Tn
    Lp = -(-L // NSA_BLOCK) * NSA_BLOCK
    nq, nkv, ng, dq, dk, dv, mg = project(h, P['w_in'])
    lam = diff_lambda(P['lambda_qk'], lam_init)
    nsl = alibi_slopes(NSA_HEADS).reshape(NSA_KV_HEADS, NSA_GROUP)
    dsl = alibi_slopes(DIFF_HEADS)
    qpos = past + jnp.arange(Tn, dtype=jnp.int32)
    new_nsa = nkv[:, :, :4]
    full = jnp.concatenate([c_nsa[page_table].reshape(Bd, past, 4, NSA_KV_HEADS, HD), new_nsa], axis=1)
    full = jnp.pad(full, ((0, 0), (0, Lp - L), (0, 0), (0, 0), (0, 0)))
    ck, cv, ks_b, vs_b = nsa_prepare(full, P['cmp_w1'], P['cmp_b1'], P['cmp_w2'])
    o_c, o_s = nsa_cmp_sel(nq, qpos, ck, cv, ks_b, vs_b, nsl)
    wb = win.shape[1]
    kw = jnp.concatenate([win[:, :, 0], nkv[:, :, 4]], axis=1)
    vw = jnp.concatenate([win[:, :, 1], nkv[:, :, 5]], axis=1)
    kpos_w = past - wb + jnp.arange(wb + Tn, dtype=jnp.int32)
    o_w = window_attn(nq, qpos, kw, vw, kpos_w, nsl)
    o_nsa = gate_nsa(ng, o_c, o_s, o_w).reshape(Bd, Tn, NSA_HEADS * HD)
    new_diff = jnp.stack([dk.reshape(Bd, Tn, DIFF_HEADS, 2 * HD), dv], axis=2)
    full_d = jnp.concatenate([c_diff[page_table].reshape(Bd, past, 2, DIFF_HEADS, 2 * HD), new_diff], axis=1)
    kd = full_d[:, :, 0].reshape(Bd, L, DIFF_HEADS, 2, HD)
    vd = full_d[:, :, 1]
    o_d = diff_attn(dq, kd, vd, qpos, jnp.arange(L, dtype=jnp.int32), lam, dsl)
    out = merge(o_nsa, o_d, mg, P, lam_init)
    new_win = jnp.concatenate([win, nkv[:, :, 4:]], axis=1)[:, Tn:]
    return out, (new_nsa, new_diff, new_win)


def sandwich_layer(x, mix_fn, P):
    f1 = swiglu(rmsnorm(x, P['g_ffn1_pre']), P['ffn1_w1'], P['ffn1_w3'], P['ffn1_w2'])
    x = x + 0.5 * rmsnorm(f1, P['g_ffn1_post'])
    o, st = mix_fn(rmsnorm(x, P['g_mix_pre']))
    x = x + rmsnorm(o, P['g_mix_post'])
    f2 = swiglu(rmsnorm(x, P['g_ffn2_pre']), P['ffn2_w1'], P['ffn2_w3'], P['ffn2_w2'])
    x = x + 0.5 * rmsnorm(f2, P['g_ffn2_post'])
    return x, st


def setup_inputs(seed: int = 0) -> dict:
    key = jax.random.key(seed)
    k = jax.random.split(key, 40)
    f32 = jnp.float32

    def nrm(i, shape, scale):
        return scale * jax.random.normal(k[i], shape, f32)

    def gain(i, n):
        return 1.0 + 0.05 * jax.random.normal(k[i], (DEPTH, n), f32)

    n_pages = PAST_LEN // PAGE_SIZE
    n_used = DEC_BATCH * n_pages
    n_pool = (5 * n_used + 3) // 4
    page_table = jax.random.permutation(k[3], n_pool)[:n_used].reshape(DEC_BATCH, n_pages).astype(jnp.int32)
    win_buf = min(NSA_WINDOW, PAST_LEN)
    return {
        'x_prompt': nrm(0, (BATCH, SEQ, D_MODEL), 1.0),
        'x_sample': nrm(1, (DEC_BATCH, DEC_SEQ, D_MODEL), 1.0),
        'cache_nsa_kv': nrm(2, (DEPTH, n_pool, PAGE_SIZE, 4, NSA_KV_HEADS, HD), 1.0),
        'cache_diff_kv': nrm(4, (DEPTH, n_pool, PAGE_SIZE, 2, DIFF_HEADS, 2 * HD), 1.0),
        'state_win_kv': nrm(5, (DEPTH, DEC_BATCH, win_buf, 2, NSA_KV_HEADS, HD), 1.0),
        'page_table': page_table,
        'g_ffn1_pre': gain(6, D_MODEL),
        'ffn1_w1': nrm(7, (DEPTH, D_MODEL, D_FF), D_MODEL ** -0.5),
        'ffn1_w3': nrm(8, (DEPTH, D_MODEL, D_FF), D_MODEL ** -0.5),
        'ffn1_w2': nrm(9, (DEPTH, D_FF, D_MODEL), D_FF ** -0.5),
        'g_ffn1_post': gain(10, D_MODEL),
        'g_mix_pre': gain(11, D_MODEL),
        'w_in': nrm(12, (DEPTH, D_MODEL, IN_WIDTH), D_MODEL ** -0.5),
        'cmp_w1': nrm(13, (DEPTH, 2, NSA_BLOCK, HD, HD), (NSA_BLOCK * HD) ** -0.5),
        'cmp_b1': nrm(14, (DEPTH, 2, HD), 0.02),
        'cmp_w2': nrm(15, (DEPTH, 2, HD, HD), HD ** -0.5),
        'lambda_qk': nrm(16, (DEPTH, 4, HD), 0.1),
        'g_diff_subln': gain(17, DIFF_VD),
        'w_up_nsa': nrm(18, (DEPTH, NSA_HEADS * HD, D_MODEL), (NSA_HEADS * HD) ** -0.5),
        'w_up_diff': nrm(19, (DEPTH, DIFF_HEADS * DIFF_VD, D_MODEL), (DIFF_HEADS * DIFF_VD) ** -0.5),
        'w_out': nrm(20, (DEPTH, D_MODEL, D_MODEL), D_MODEL ** -0.5),
        'g_mix_post': gain(21, D_MODEL),
        'g_ffn2_pre': gain(22, D_MODEL),
        'ffn2_w1': nrm(23, (DEPTH, D_MODEL, D_FF), D_MODEL ** -0.5),
        'ffn2_w3': nrm(24, (DEPTH, D_MODEL, D_FF), D_MODEL ** -0.5),
        'ffn2_w2': nrm(25, (DEPTH, D_FF, D_MODEL), D_FF ** -0.5),
        'g_ffn2_post': gain(26, D_MODEL),
    }


def reference(x_prompt, x_sample, cache_nsa_kv, cache_diff_kv, state_win_kv, page_table,
              g_ffn1_pre, ffn1_w1, ffn1_w3, ffn1_w2, g_ffn1_post,
              g_mix_pre, w_in, cmp_w1, cmp_b1, cmp_w2, lambda_qk, g_diff_subln,
              w_up_nsa, w_up_diff, w_out, g_mix_post,
              g_ffn2_pre, ffn2_w1, ffn2_w3, ffn2_w2, g_ffn2_post):
    y_p, y_s = x_prompt, x_sample
    nsa_p, diff_p, win_p, nsa_s, diff_s, win_s = [], [], [], [], [], []
    for l in range(DEPTH):
        lam_init = 0.8 - 0.6 * math.exp(-0.3 * l)
        P = {
            'g_ffn1_pre': g_ffn1_pre[l], 'ffn1_w1': ffn1_w1[l], 'ffn1_w3': ffn1_w3[l],
            'ffn1_w2': ffn1_w2[l], 'g_ffn1_post': g_ffn1_post[l],
            'g_mix_pre': g_mix_pre[l], 'w_in': w_in[l], 'cmp_w1': cmp_w1[l],
            'cmp_b1': cmp_b1[l], 'cmp_w2': cmp_w2[l], 'lambda_qk': lambda_qk[l],
            'g_diff_subln': g_diff_subln[l], 'w_up_nsa': w_up_nsa[l],
            'w_up_diff': w_up_diff[l], 'w_out': w_out[l], 'g_mix_post': g_mix_post[l],
            'g_ffn2_pre': g_ffn2_pre[l], 'ffn2_w1': ffn2_w1[l], 'ffn2_w3': ffn2_w3[l],
            'ffn2_w2': ffn2_w2[l], 'g_ffn2_post': g_ffn2_post[l],
        }
        y_p, (a_p, b_p, c_p) = sandwich_layer(y_p, lambda h: prompt_mix(h, P, lam_init), P)
        y_s, (a_s, b_s, c_s) = sandwich_layer(
            y_s, lambda h: sample_mix(h, cache_nsa_kv[l], cache_diff_kv[l], state_win_kv[l],
                                      page_table, P, lam_init), P)
        nsa_p.append(a_p); diff_p.append(b_p); win_p.append(c_p)
        nsa_s.append(a_s); diff_s.append(b_s); win_s.append(c_s)
    return (y_p, y_s, jnp.stack(nsa_p), jnp.stack(nsa_s), jnp.stack(diff_p), jnp.stack(diff_s),
            jnp.stack(win_p), jnp.stack(win_s))
```

```python
import functools
import math

import jax
import jax.numpy as jnp
from jax import lax
from jax.experimental import pallas as pl
from jax.experimental.pallas import tpu as pltpu

F32 = jnp.float32
BF16 = jnp.bfloat16

HD = 64
NSA_HEADS = 8
NSA_KV_HEADS = 2
NSA_GROUP = NSA_HEADS // NSA_KV_HEADS
NSA_BLOCK = 64
NSA_TOPN = 16
NSA_WINDOW = 512
DIFF_HEADS = 4
DIFF_VD = 2 * HD
EPS = 1e-6
FORCE_SCORE = 1e4
NEG = -1e30
LANES = 128
FF_CHUNK = 256
VMEM_LIMIT = 56 * 1024 * 1024

NQ_W = NSA_HEADS * HD
NKV_W = 6 * NSA_KV_HEADS * HD
NG_W = 3 * NSA_HEADS
DQ_W = DIFF_HEADS * 2 * HD
DV_W = DIFF_HEADS * DIFF_VD


def _nsa_slope(head):
    return 2.0 ** (-8.0 * (head + 1) / NSA_HEADS)


def _dot(a, b):
    return jnp.dot(a, b, preferred_element_type=F32)


def _dot_nt(a, b):
    return lax.dot_general(a, b, (((1,), (1,)), ((), ())), preferred_element_type=F32)


def _rms(x, g):
    return x * lax.rsqrt(jnp.mean(x * x, axis=-1, keepdims=True) + EPS) * g


def _gelu_tanh(x):
    c = math.sqrt(2.0 / math.pi)
    return 0.5 * x * (1.0 + jnp.tanh(c * (x + 0.044715 * (x * x * x))))


def _const_spec(shape):
    nd = len(shape)
    return pl.BlockSpec(shape, lambda *_: (0,) * nd, pipeline_mode=pl.Buffered(1))


def _params(sem):
    return pltpu.CompilerParams(dimension_semantics=sem, vmem_limit_bytes=VMEM_LIMIT)


def _ffn_half_step(x, g_pre, w1_ref, w3_ref, w2_ref, g_post):
    xn = _rms(x, g_pre).astype(BF16)
    d_ff = w1_ref.shape[1]
    acc = jnp.zeros(x.shape, F32)
    for c in range(d_ff // FF_CHUNK):
        cs = slice(c * FF_CHUNK, (c + 1) * FF_CHUNK)
        h1 = _dot(xn, w1_ref[:, cs])
        h3 = _dot(xn, w3_ref[:, cs])
        gate = (h1 * jax.nn.sigmoid(h1) * h3).astype(BF16)
        acc = acc + _dot(gate, w2_ref[cs, :])
    return x + 0.5 * _rms(acc, g_post)


def _ffn_proj_kernel(x_ref, gpre_ref, w1_ref, w3_ref, w2_ref, gpost_ref, gmix_ref, wp_ref,
                     x1_ref, q_ref, selwin_ref, cmp_ref, nsa_new_ref, win_new_ref,
                     diff_new_ref, dq_ref, dk_ref, dv_ref, ng_ref):
    x1 = _ffn_half_step(x_ref[...], gpre_ref[...], w1_ref, w3_ref, w2_ref, gpost_ref[...])
    x1_ref[...] = x1
    h = _rms(x1, gmix_ref[...]).astype(BF16)
    scale = HD ** -0.5
    o = 0
    nq = _dot(h, wp_ref[:, o:o + NQ_W]); o += NQ_W
    for i in range(NSA_HEADS):
        q_ref[i] = (nq[:, i * HD:(i + 1) * HD] * scale).astype(BF16)
    nkv = _dot(h, wp_ref[:, o:o + NKV_W]); o += NKV_W
    nsa_new_ref[...] = nkv[:, :4 * NSA_KV_HEADS * HD]
    win_new_ref[...] = nkv[:, 4 * NSA_KV_HEADS * HD:]
    for i in range(2 * NSA_KV_HEADS):
        cmp_ref[i] = nkv[:, i * HD:(i + 1) * HD].astype(BF16)
    for i in range(4 * NSA_KV_HEADS):
        c0 = (2 * NSA_KV_HEADS + i) * HD
        selwin_ref[i] = nkv[:, c0:c0 + HD].astype(BF16)
    dq = _dot(h, wp_ref[:, o:o + DQ_W]); o += DQ_W
    for i in range(2 * DIFF_HEADS):
        dq_ref[i] = (dq[:, i * HD:(i + 1) * HD] * scale).astype(BF16)
    dk = _dot(h, wp_ref[:, o:o + DQ_W]); o += DQ_W
    diff_new_ref[:, :DQ_W] = dk
    for i in range(2 * DIFF_HEADS):
        dk_ref[i] = dk[:, i * HD:(i + 1) * HD].astype(BF16)
    dv = _dot(h, wp_ref[:, o:o + DV_W]); o += DV_W
    diff_new_ref[:, DQ_W:] = dv
    for i in range(DIFF_HEADS):
        dv_ref[i] = dv[:, i * DIFF_VD:(i + 1) * DIFF_VD].astype(BF16)
    ng_ref[...] = jax.nn.sigmoid(_dot(h, wp_ref[:, o:o + LANES]))


def _ffn_proj(x, gpre, w1, w3, w2, gpost, gmix, wp, tm):
    n, d = x.shape
    d_ff = w1.shape[1]
    row = lambda w: pl.BlockSpec((tm, w), lambda i: (i, 0))
    sep = lambda k, w: pl.BlockSpec((k, tm, w), lambda i: (0, i, 0))
    out_shape = (
        jax.ShapeDtypeStruct((n, d), F32),
        jax.ShapeDtypeStruct((NSA_HEADS, n, HD), BF16),
        jax.ShapeDtypeStruct((4 * NSA_KV_HEADS, n, HD), BF16),
        jax.ShapeDtypeStruct((2 * NSA_KV_HEADS, n, HD), BF16),
        jax.ShapeDtypeStruct((n, 4 * NSA_KV_HEADS * HD), F32),
        jax.ShapeDtypeStruct((n, 2 * NSA_KV_HEADS * HD), F32),
        jax.ShapeDtypeStruct((n, DQ_W + DV_W), F32),
        jax.ShapeDtypeStruct((2 * DIFF_HEADS, n, HD), BF16),
        jax.ShapeDtypeStruct((2 * DIFF_HEADS, n, HD), BF16),
        jax.ShapeDtypeStruct((DIFF_HEADS, n, DIFF_VD), BF16),
        jax.ShapeDtypeStruct((n, LANES), F32),
    )
    out_specs = (row(d), sep(NSA_HEADS, HD), sep(4 * NSA_KV_HEADS, HD), sep(2 * NSA_KV_HEADS, HD),
                 row(4 * NSA_KV_HEADS * HD), row(2 * NSA_KV_HEADS * HD), row(DQ_W + DV_W),
                 sep(2 * DIFF_HEADS, HD), sep(2 * DIFF_HEADS, HD), sep(DIFF_HEADS, DIFF_VD), row(LANES))
    in_specs = [row(d), _const_spec((1, d)), _const_spec((d, d_ff)), _const_spec((d, d_ff)),
                _const_spec((d_ff, d)), _const_spec((1, d)), _const_spec((1, d)),
                _const_spec(wp.shape)]
    return pl.pallas_call(
        _ffn_proj_kernel, grid=(n // tm,), in_specs=in_specs, out_specs=out_specs,
        out_shape=out_shape, compiler_params=_params(("parallel",)), name="ffn_proj",
    )(x, gpre, w1, w3, w2, gpost, gmix, wp)


def _compress_kernel(x_ref, w1_ref, b1_ref, w2_ref, o_ref):
    hid = _gelu_tanh(_dot(x_ref[0], w1_ref[0]) + b1_ref[0])
    o_ref[0] = _dot(hid.astype(BF16), w2_ref[0])


def _compress_prompt(xc, w1r, b1, w2):
    k, nb, kd = xc.shape
    return pl.pallas_call(
        _compress_kernel, grid=(k,),
        in_specs=[pl.BlockSpec((1, nb, kd), lambda i: (i, 0, 0)),
                  pl.BlockSpec((1, kd, HD), lambda i: (i // NSA_KV_HEADS, 0, 0)),
                  pl.BlockSpec((1, 1, HD), lambda i: (i // NSA_KV_HEADS, 0, 0)),
                  pl.BlockSpec((1, HD, HD), lambda i: (i // NSA_KV_HEADS, 0, 0))],
        out_specs=pl.BlockSpec((1, nb, HD), lambda i: (i, 0, 0)),
        out_shape=jax.ShapeDtypeStruct((k, nb, HD), F32),
        compiler_params=_params(("parallel",)), name="nsa_compress_prompt",
    )(xc, w1r, b1, w2)


NSA_TQ = 128
NSA_TK = 256


def _topn_mask_t(imp_t, nb):
    nt = imp_t.shape[1]
    rank = jnp.zeros((nb, nt), F32)
    vals = imp_t[:nb]
    jidx = lax.broadcasted_iota(jnp.int32, (nb, nt), 0)
    for i in range(nb):
        vi = jnp.broadcast_to(imp_t[i:i + 1, :], (nb, nt))
        ahead = (vi > vals) | ((vi == vals) & (jidx > i))
        rank = rank + jnp.where(ahead, 1.0, 0.0)
    return jnp.where((rank < NSA_TOPN) & (vals >= 0.0), 1.0, 0.0)


def _nsa_prompt_kernel(q_ref, ckcv_ref, sw_ref, ng_ref, o_ref):
    qi = pl.program_id(1)
    s0 = qi * NSA_TQ
    nb = ckcv_ref.shape[1]
    rows = NSA_GROUP * NSA_TQ
    tok = lax.broadcasted_iota(jnp.int32, (rows, 1), 0) % NSA_TQ
    qpos = s0 + tok
    gidx = lax.broadcasted_iota(jnp.int32, (rows, 1), 0) // NSA_TQ
    ng = ng_ref[...]
    for kvh in range(NSA_KV_HEADS):
        slope = jnp.zeros((rows, 1), F32)
        for g in range(NSA_GROUP):
            slope = jnp.where(gidx == g, _nsa_slope(kvh * NSA_GROUP + g), slope)
        q = q_ref[kvh * NSA_GROUP:(kvh + 1) * NSA_GROUP].reshape(rows, HD)

        ck = ckcv_ref[kvh].astype(BF16)
        cv = ckcv_ref[NSA_KV_HEADS + kvh].astype(BF16)
        blk = lax.broadcasted_iota(jnp.int32, (1, nb), 1)
        dist_c = (qpos - ((blk + 1) * NSA_BLOCK - 1)).astype(F32)
        valid_c = dist_c >= 0.0
        s = _dot_nt(q, ck) - slope * dist_c
        s = jnp.where(valid_c, s, NEG)
        p = jnp.where(valid_c, jnp.exp(s - jnp.max(s, axis=-1, keepdims=True)), 0.0)
        p = p / jnp.maximum(jnp.sum(p, axis=-1, keepdims=True), 1e-30)
        o_c = _dot(p.astype(BF16), cv)

        imp = p[0:NSA_TQ]
        for g in range(1, NSA_GROUP):
            imp = imp + p[g * NSA_TQ:(g + 1) * NSA_TQ]
        tq = s0 + lax.broadcasted_iota(jnp.int32, (NSA_TQ, 1), 0)
        cur = tq // NSA_BLOCK
        forced = (blk == cur) | (blk == 0)
        imp = jnp.where(forced, FORCE_SCORE, jnp.where(blk <= cur, imp, -1.0))
        pad = jnp.full((NSA_TQ, LANES - nb), -1.0, F32)
        imp_t = jnp.concatenate([imp, pad], axis=1).T
        sel_t = _topn_mask_t(imp_t, nb)
        sel_t = jnp.concatenate([sel_t, jnp.zeros((LANES - nb, NSA_TQ), F32)], axis=0)
        sel = sel_t.T.astype(BF16)

        def sel_step(kt, carry):
            m, l, acc = carry
            k0 = pl.multiple_of(kt * NSA_TK, NSA_TK)
            k = sw_ref[kvh, pl.ds(k0, NSA_TK), :]
            v = sw_ref[NSA_KV_HEADS + kvh, pl.ds(k0, NSA_TK), :]
            kpos = k0 + lax.broadcasted_iota(jnp.int32, (1, NSA_TK), 1)
            expand = (lax.broadcasted_iota(jnp.int32, (LANES, NSA_TK), 0)
                      == kpos // NSA_BLOCK).astype(BF16)
            hit = _dot(sel, expand)
            hit = jnp.concatenate([hit] * NSA_GROUP, axis=0)
            dist = qpos - kpos
            valid = (hit > 0.5) & (dist >= 0)
            sc = _dot_nt(q, k) - slope * dist.astype(F32)
            sc = jnp.where(valid, sc, NEG)
            m_new = jnp.maximum(m, jnp.max(sc, axis=-1, keepdims=True))
            pr = jnp.where(valid, jnp.exp(sc - m_new), 0.0)
            alpha = jnp.exp(m - m_new)
            l = alpha * l + jnp.sum(pr, axis=-1, keepdims=True)
            acc = alpha * acc + _dot(pr.astype(BF16), v)
            return m_new, l, acc

        n_tiles = (s0 + NSA_TQ + NSA_TK - 1) // NSA_TK
        init = (jnp.full((rows, 1), NEG, F32), jnp.zeros((rows, 1), F32), jnp.zeros((rows, HD), F32))
        _, l_s, acc_s = lax.fori_loop(0, n_tiles, sel_step, init)
        o_s = acc_s / jnp.maximum(l_s, 1e-30)

        span = NSA_WINDOW + NSA_TQ
        w0 = pl.multiple_of(jnp.maximum(s0 - NSA_WINDOW, 0), NSA_TQ)
        kw = sw_ref[2 * NSA_KV_HEADS + kvh, pl.ds(w0, span), :]
        vw = sw_ref[3 * NSA_KV_HEADS + kvh, pl.ds(w0, span), :]
        dist = qpos - (w0 + lax.broadcasted_iota(jnp.int32, (1, span), 1))
        valid = (dist >= 0) & (dist < NSA_WINDOW)
        sc = _dot_nt(q, kw) - slope * dist.astype(F32)
        sc = jnp.where(valid, sc, NEG)
        pr = jnp.where(valid, jnp.exp(sc - jnp.max(sc, axis=-1, keepdims=True)), 0.0)
        pr = pr / jnp.maximum(jnp.sum(pr, axis=-1, keepdims=True), 1e-30)
        o_w = _dot(pr.astype(BF16), vw)

        for g in range(NSA_GROUP):
            hd_i = kvh * NSA_GROUP + g
            r = slice(g * NSA_TQ, (g + 1) * NSA_TQ)
            out = (ng[:, 3 * hd_i:3 * hd_i + 1] * o_c[r] + ng[:, 3 * hd_i + 1:3 * hd_i + 2] * o_s[r]
                   + ng[:, 3 * hd_i + 2:3 * hd_i + 3] * o_w[r])
            o_ref[:, hd_i * HD:(hd_i + 1) * HD] = out.astype(o_ref.dtype)


def _nsa_prompt(q_sep, ckcv, selwin, ng, bsz, t):
    nq = t // NSA_TQ
    nb = t // NSA_BLOCK
    n = bsz * t
    return pl.pallas_call(
        _nsa_prompt_kernel, grid=(bsz, nq),
        in_specs=[pl.BlockSpec((NSA_HEADS, NSA_TQ, HD), lambda b, i: (0, b * nq + i, 0)),
                  pl.BlockSpec((2 * NSA_KV_HEADS, nb, HD), lambda b, i: (0, b, 0)),
                  pl.BlockSpec((4 * NSA_KV_HEADS, t, HD), lambda b, i: (0, b, 0)),
                  pl.BlockSpec((NSA_TQ, LANES), lambda b, i: (b * nq + i, 0))],
        out_specs=pl.BlockSpec((NSA_TQ, NQ_W), lambda b, i: (b * nq + i, 0)),
        out_shape=jax.ShapeDtypeStruct((n, NQ_W), BF16),
        compiler_params=_params(("parallel", "parallel")), name="nsa_prompt",
    )(q_sep, ckcv, selwin, ng)


DIFF_TQ = 256
DIFF_TK = 256


def _diff_lambda(lqk_ref, lam_init):
    l = lqk_ref[...]
    a = jnp.sum(l[0:1] * l[1:2], axis=-1, keepdims=True)
    b = jnp.sum(l[2:3] * l[3:4], axis=-1, keepdims=True)
    return jnp.exp(a) - jnp.exp(b) + lam_init


def _pow2_neg(n_vec):
    return lax.bitcast_convert_type((127 - n_vec) << 23, F32)


def _diff_prompt_kernel(lam_init, q_ref, k_ref, v_ref, lqk_ref, gsub_ref, o_ref):
    h = pl.program_id(1)
    qi = pl.program_id(2)
    s0 = qi * DIFF_TQ
    slope = _pow2_neg(jnp.full((1, 1), 2 * (h + 1), jnp.int32))
    rel = (lax.broadcasted_iota(jnp.int32, (DIFF_TQ, DIFF_TK), 0)
           - lax.broadcasted_iota(jnp.int32, (DIFF_TQ, DIFF_TK), 1))
    q1 = q_ref[0]
    q2 = q_ref[1]

    def step(kt, carry):
        m1, l1, m2, l2, acc = carry
        k0 = pl.multiple_of(kt * DIFF_TK, DIFF_TK)
        dist = rel + (s0 - k0)
        valid = dist >= 0
        bias = slope * dist.astype(F32)
        v = v_ref[0, pl.ds(k0, DIFF_TK), :]

        def one(q, k, m, l):
            sc = jnp.where(valid, _dot_nt(q, k) - bias, NEG)
            m_new = jnp.maximum(m, jnp.max(sc, axis=-1, keepdims=True))
            pr = jnp.where(valid, jnp.exp(sc - m_new), 0.0)
            alpha = jnp.exp(m - m_new)
            return m_new, alpha * l + jnp.sum(pr, axis=-1, keepdims=True), alpha, pr

        m1, l1, a1, p1 = one(q1, k_ref[0, pl.ds(k0, DIFF_TK), :], m1, l1)
        m2, l2, a2, p2 = one(q2, k_ref[1, pl.ds(k0, DIFF_TK), :], m2, l2)
        pv = _dot(jnp.concatenate([p1, p2], axis=0).astype(BF16), v)
        acc = jnp.concatenate([a1, a2], axis=0) * acc + pv
        return m1, l1, m2, l2, acc

    col = lambda val: jnp.full((DIFF_TQ, 1), val, F32)
    init = (col(NEG), col(0.0), col(NEG), col(0.0), jnp.zeros((2 * DIFF_TQ, DIFF_VD), F32))
    n_tiles = (s0 + DIFF_TQ + DIFF_TK - 1) // DIFF_TK
    _, l1, _, l2, acc = lax.fori_loop(0, n_tiles, step, init)
    lam = _diff_lambda(lqk_ref, lam_init)
    o = (acc[:DIFF_TQ] / jnp.maximum(l1, 1e-30)
         - lam * (acc[DIFF_TQ:] / jnp.maximum(l2, 1e-30)))
    o_ref[...] = (_rms(o, gsub_ref[...]) * (1.0 - lam_init)).astype(o_ref.dtype)


def _diff_prompt(dq, dk, dv, lqk, gsub, lam_init, bsz, t):
    nq = t // DIFF_TQ
    n = bsz * t
    return pl.pallas_call(
        functools.partial(_diff_prompt_kernel, lam_init), grid=(bsz, DIFF_HEADS, nq),
        in_specs=[pl.BlockSpec((2, DIFF_TQ, HD), lambda b, h, i: (h, b * nq + i, 0)),
                  pl.BlockSpec((2, t, HD), lambda b, h, i: (h, b, 0)),
                  pl.BlockSpec((1, t, DIFF_VD), lambda b, h, i: (h, b, 0)),
                  pl.BlockSpec(lqk.shape, lambda b, h, i: (0, 0)),
                  pl.BlockSpec(gsub.shape, lambda b, h, i: (0, 0))],
        out_specs=pl.BlockSpec((DIFF_TQ, DIFF_VD), lambda b, h, i: (b * nq + i, h)),
        out_shape=jax.ShapeDtypeStruct((n, DV_W), BF16),
        compiler_params=_params(("parallel", "parallel", "parallel")), name="diff_prompt",
    )(dq, dk, dv, lqk, gsub)


def _merge_ffn_kernel(x_ref, on_ref, od_ref, gmix_ref, wmg_ref, wun_ref, wud_ref, wout_ref,
                      gmixpost_ref, gpre_ref, w1_ref, w3_ref, w2_ref, gpost_ref, y_ref):
    x1 = x_ref[...]
    d = x1.shape[1]
    h = _rms(x1, gmix_ref[...]).astype(BF16)
    up_n = _dot(on_ref[...].astype(BF16), wun_ref[...])
    up_d = _dot(od_ref[...].astype(BF16), wud_ref[...])
    m = (jax.nn.sigmoid(_dot(h, wmg_ref[:, :d])) * up_n
         + jax.nn.sigmoid(_dot(h, wmg_ref[:, d:])) * up_d)
    out = _dot(m.astype(BF16), wout_ref[...])
    x2 = x1 + _rms(out, gmixpost_ref[...])
    y_ref[...] = _ffn_half_step(x2, gpre_ref[...], w1_ref, w3_ref, w2_ref, gpost_ref[...])


def _merge_ffn(x1, o_nsa, o_diff, gmix, wmg, wun, wud, wout, gmixpost, gpre, w1, w3, w2, gpost, tm):
    n, d = x1.shape
    d_ff = w1.shape[1]
    row = lambda w: pl.BlockSpec((tm, w), lambda i: (i, 0))
    in_specs = [row(d), row(NQ_W), row(DV_W), _const_spec((1, d)), _const_spec(wmg.shape),
                _const_spec(wun.shape), _const_spec(wud.shape), _const_spec(wout.shape),
                _const_spec((1, d)), _const_spec((1, d)), _const_spec((d, d_ff)),
                _const_spec((d, d_ff)), _const_spec((d_ff, d)), _const_spec((1, d))]
    return pl.pallas_call(
        _merge_ffn_kernel, grid=(n // tm,), in_specs=in_specs, out_specs=row(d),
        out_shape=jax.ShapeDtypeStruct((n, d), F32),
        compiler_params=_params(("parallel",)), name="merge_ffn",
    )(x1, o_nsa, o_diff, gmix, wmg, wun, wud, wout, gmixpost, gpre, w1, w3, w2, gpost)


CMP_PAGES = 16
SEL_PAGES = 16
DIFF_PAGES = 8
CK_ROWS = 384


def _page_specs(n_pages, page_rows, width, half, pages_per_seq):
    def make(i):
        def imap(b, j, pt):
            return (pt[b * pages_per_seq + j * n_pages + i], 0, half)
        return pl.BlockSpec((None, page_rows, width), imap)
    return [make(i) for i in range(n_pages)]


def _compress_sample_kernel(nb_pad, n_steps, pt_ref, *refs):
    pages = refs[:CMP_PAGES]
    new_ref, w1_ref, b1_ref, w2_ref, o_ref, xs_ref = refs[CMP_PAGES:]
    j = pl.program_id(1)
    page_rows = pages[0].shape[0]
    past = n_steps * CMP_PAGES * page_rows
    tail = xs_ref.shape[1] - past

    @pl.when(j == 0)
    def _():
        tn = new_ref.shape[1]
        for s in range(2):
            xs_ref[s, pl.ds(past, tn), :] = new_ref[0, :, s * LANES:(s + 1) * LANES]
            xs_ref[s, pl.ds(past + tn, tail - tn), :] = jnp.zeros((tail - tn, LANES), F32)

    for i, pg in enumerate(pages):
        r0 = pl.multiple_of((j * CMP_PAGES + i) * page_rows, page_rows)
        for s in range(2):
            xs_ref[s, pl.ds(r0, page_rows), :] = pg[:, s * LANES:(s + 1) * LANES]

    @pl.when(j == n_steps - 1)
    def _():
        for s in range(2):
            def body(p, acc):
                lhs = xs_ref[s, pl.ds(p, nb_pad, stride=NSA_BLOCK), :].astype(BF16)
                return acc + _dot(lhs, w1_ref[s, p])
            acc = lax.fori_loop(0, NSA_BLOCK, body, jnp.zeros((nb_pad, LANES), F32))
            hid = _gelu_tanh(acc + b1_ref[s]).astype(BF16)
            o_ref[0, s, :nb_pad, :] = _dot(hid, w2_ref[s])
            o_ref[0, s, nb_pad:, :] = jnp.zeros((CK_ROWS - nb_pad, LANES), F32)


def _compress_sample(page_table, cache3, new_cmp, bd1, b1t, bd2):
    bd, pps = page_table.shape
    page_rows = cache3.shape[1]
    tn = new_cmp.shape[1]
    past = pps * page_rows
    nb = -(-(past + tn) // NSA_BLOCK)
    nb_pad = -(-nb // 8) * 8
    n_steps = pps // CMP_PAGES
    in_specs = _page_specs(CMP_PAGES, page_rows, 2 * LANES, 0, pps) + [
        pl.BlockSpec((1, tn, 2 * LANES), lambda b, j, pt: (b, 0, 0)),
        pl.BlockSpec(bd1.shape, lambda b, j, pt: (0, 0, 0, 0), pipeline_mode=pl.Buffered(1)),
        pl.BlockSpec(b1t.shape, lambda b, j, pt: (0, 0, 0), pipeline_mode=pl.Buffered(1)),
        pl.BlockSpec(bd2.shape, lambda b, j, pt: (0, 0, 0), pipeline_mode=pl.Buffered(1))]
    grid_spec = pltpu.PrefetchScalarGridSpec(
        num_scalar_prefetch=1, grid=(bd, n_steps), in_specs=in_specs,
        out_specs=pl.BlockSpec((1, 2, CK_ROWS, LANES), lambda b, j, pt: (b, 0, 0, 0)),
        scratch_shapes=[pltpu.VMEM((2, nb_pad * NSA_BLOCK, LANES), F32)])
    return pl.pallas_call(
        functools.partial(_compress_sample_kernel, nb_pad, n_steps), grid_spec=grid_spec,
        out_shape=jax.ShapeDtypeStruct((bd, 2, CK_ROWS, LANES), F32),
        compiler_params=_params(("arbitrary", "arbitrary")), name="nsa_compress_sample",
    )(page_table.reshape(-1), *([cache3] * CMP_PAGES), new_cmp, bd1, b1t, bd2)


def _head_diag(full, n_heads, rows_per_head, width):
    return jnp.concatenate(
        [full[h * rows_per_head:(h + 1) * rows_per_head, h * width:(h + 1) * width]
         for h in range(n_heads)], axis=0)


def _nsa_sample_kernel(tn, nb, n_steps, pt_ref, *refs):
    pages = refs[:SEL_PAGES]
    (qbd_ref, ckcv_ref, new_ref, win_ref, wnew_ref, gate_ref, o_ref,
     sel_ref, oc_ref, m_ref, l_ref, acc_ref) = refs[SEL_PAGES:]
    j = pl.program_id(1)
    page_rows = pages[0].shape[0]
    past = n_steps * SEL_PAGES * page_rows
    rows = NSA_HEADS * tn
    rpk = NSA_GROUP * tn
    ridx = lax.broadcasted_iota(jnp.int32, (rows, 1), 0)
    qpos = past + ridx % tn
    head = ridx // tn
    slope = jnp.zeros((rows, 1), F32)
    for hd_i in range(NSA_HEADS):
        slope = jnp.where(head == hd_i, _nsa_slope(hd_i), slope)
    qbd = qbd_ref[0]

    @pl.when(j == 0)
    def _():
        ck = ckcv_ref[0, 0].astype(BF16)
        cv = ckcv_ref[0, 1].astype(BF16)
        blk = lax.broadcasted_iota(jnp.int32, (1, CK_ROWS), 1)
        dist_c = (qpos - ((blk + 1) * NSA_BLOCK - 1)).astype(F32)
        valid_c = (dist_c >= 0.0) & (blk < nb)
        s = jnp.where(valid_c, _dot_nt(qbd, ck) - slope * dist_c, NEG)
        p = jnp.where(valid_c, jnp.exp(s - jnp.max(s, axis=-1, keepdims=True)), 0.0)
        p = p / jnp.maximum(jnp.sum(p, axis=-1, keepdims=True), 1e-30)
        oc_ref[...] = _head_diag(_dot(p.astype(BF16), cv), NSA_KV_HEADS, rpk, HD)
        imps = []
        for kvh in range(NSA_KV_HEADS):
            imp = p[kvh * rpk:kvh * rpk + tn]
            for g in range(1, NSA_GROUP):
                imp = imp + p[kvh * rpk + g * tn:kvh * rpk + (g + 1) * tn]
            imps.append(imp)
        imp = jnp.concatenate(imps, axis=0)
        tq = past + lax.broadcasted_iota(jnp.int32, (NSA_KV_HEADS * tn, 1), 0) % tn
        cur = tq // NSA_BLOCK
        forced = (blk == cur) | (blk == 0)
        imp = jnp.where(forced, FORCE_SCORE, jnp.where((blk <= cur) & (blk < nb), imp, -1.0))
        imp = jnp.where(blk < nb, imp, -3.0)

        def pick(_, carry):
            vals, sel = carry
            top = jnp.max(vals, axis=-1, keepdims=True)
            first = jnp.min(jnp.where(vals == top, blk, CK_ROWS), axis=-1, keepdims=True)
            hit = blk == first
            sel = jnp.where(hit & (top >= 0.0), 1.0, sel)
            return jnp.where(hit, -2.0, vals), sel

        _, sel = lax.fori_loop(0, min(NSA_TOPN, nb), pick, (imp, jnp.zeros(imp.shape, F32)))
        sel_rows = jnp.concatenate(
            [sel[kvh * tn:(kvh + 1) * tn] for kvh in range(NSA_KV_HEADS) for _ in range(NSA_GROUP)],
            axis=0)
        for tile in range(CK_ROWS // LANES):
            sel_ref[tile] = sel_rows[:, tile * LANES:(tile + 1) * LANES]
        m_ref[...] = jnp.full(m_ref.shape, NEG, F32)
        l_ref[...] = jnp.zeros(l_ref.shape, F32)
        acc_ref[...] = jnp.zeros(acc_ref.shape, F32)

    def attend(k, v, kpos, blk_rel, sel_tile, extra_valid):
        n = k.shape[0]
        expand = (lax.broadcasted_iota(jnp.int32, (LANES, n), 0) == blk_rel).astype(BF16)
        hit = _dot(sel_tile.astype(BF16), expand)
        dist = qpos - kpos
        valid = (hit > 0.5) & (dist >= 0) & extra_valid
        sc = jnp.where(valid, _dot_nt(qbd, k) - slope * dist.astype(F32), NEG)
        m_old = m_ref[...]
        m_new = jnp.maximum(m_old, jnp.max(sc, axis=-1, keepdims=True))
        pr = jnp.where(valid, jnp.exp(sc - m_new), 0.0)
        alpha = jnp.exp(m_old - m_new)
        m_ref[...] = m_new
        l_ref[...] = alpha * l_ref[...] + jnp.sum(pr, axis=-1, keepdims=True)
        acc_ref[...] = alpha * acc_ref[...] + _dot(pr.astype(BF16), v)

    blocks_per_page = page_rows // NSA_BLOCK
    for i, pg in enumerate(pages):
        page_idx = j * SEL_PAGES + i
        blk0 = page_idx * blocks_per_page
        tile = blk0 // LANES
        lane = lax.broadcasted_iota(jnp.int32, (1, page_rows), 1)
        kpos = page_idx * page_rows + lane
        attend(pg[:, :2 * HD].astype(BF16), pg[:, 2 * HD:].astype(BF16), kpos,
               blk0 - tile * LANES + lane // NSA_BLOCK, sel_ref[tile], True)

    @pl.when(j == n_steps - 1)
    def _():
        lane = lax.broadcasted_iota(jnp.int32, (1, LANES), 1)
        zpad = jnp.zeros((LANES - tn, 2 * HD), F32)
        newk = jnp.concatenate([new_ref[0, :, :2 * HD], zpad], axis=0).astype(BF16)
        newv = jnp.concatenate([new_ref[0, :, 2 * HD:], zpad], axis=0).astype(BF16)
        blk0 = past // NSA_BLOCK
        tile = blk0 // LANES
        attend(newk, newv, past + lane, blk0 - tile * LANES + lane // NSA_BLOCK,
               sel_ref[tile], lane < tn)
        o_s = _head_diag(acc_ref[...] / jnp.maximum(l_ref[...], 1e-30), NSA_KV_HEADS, rpk, HD)

        wb = win_ref.shape[1]
        span = -(-(wb + tn) // LANES) * LANES
        zw = jnp.zeros((span - wb - tn, 2 * HD), F32)
        kw = jnp.concatenate([win_ref[0, :, :2 * HD], wnew_ref[0, :, :2 * HD], zw], axis=0).astype(BF16)
        vw = jnp.concatenate([win_ref[0, :, 2 * HD:], wnew_ref[0, :, 2 * HD:], zw], axis=0).astype(BF16)
        lane_w = lax.broadcasted_iota(jnp.int32, (1, span), 1)
        kpos_w = past - wb + lane_w
        dist = qpos - kpos_w
        valid = (dist >= 0) & (dist < NSA_WINDOW) & (kpos_w >= 0) & (lane_w < wb + tn)
        sc = jnp.where(valid, _dot_nt(qbd, kw) - slope * dist.astype(F32), NEG)
        pr = jnp.where(valid, jnp.exp(sc - jnp.max(sc, axis=-1, keepdims=True)), 0.0)
        pr = pr / jnp.maximum(jnp.sum(pr, axis=-1, keepdims=True), 1e-30)
        o_w = _head_diag(_dot(pr.astype(BF16), vw), NSA_KV_HEADS, rpk, HD)

        gt = gate_ref[0]
        out = gt[:, 0:1] * oc_ref[...] + gt[:, 1:2] * o_s + gt[:, 2:3] * o_w
        for hd_i in range(NSA_HEADS):
            o_ref[0, :, hd_i * HD:(hd_i + 1) * HD] = out[hd_i * tn:(hd_i + 1) * tn]


def _nsa_sample(page_table, cache3, qbd, ckcv, new_sel, win, win_new, gates, nb):
    bd, pps = page_table.shape
    page_rows = cache3.shape[1]
    tn = new_sel.shape[1]
    rows = NSA_HEADS * tn
    n_steps = pps // SEL_PAGES
    per_b = lambda shape: pl.BlockSpec((1,) + shape, lambda b, j, pt: (b,) + (0,) * len(shape))
    in_specs = _page_specs(SEL_PAGES, page_rows, 2 * LANES, 1, pps) + [
        per_b((rows, 2 * HD)), per_b((2, CK_ROWS, LANES)), per_b((tn, 2 * LANES)),
        per_b(win.shape[1:]), per_b((tn, 2 * LANES)), per_b((rows, LANES))]
    grid_spec = pltpu.PrefetchScalarGridSpec(
        num_scalar_prefetch=1, grid=(bd, n_steps), in_specs=in_specs,
        out_specs=per_b((tn, NQ_W)),
        scratch_shapes=[pltpu.VMEM((CK_ROWS // LANES, rows, LANES), F32), pltpu.VMEM((rows, HD), F32),
                        pltpu.VMEM((rows, 1), F32), pltpu.VMEM((rows, 1), F32),
                        pltpu.VMEM((rows, 2 * HD), F32)])
    return pl.pallas_call(
        functools.partial(_nsa_sample_kernel, tn, nb, n_steps), grid_spec=grid_spec,
        out_shape=jax.ShapeDtypeStruct((bd, tn, NQ_W), F32),
        compiler_params=_params(("arbitrary", "arbitrary")), name="nsa_sample",
    )(page_table.reshape(-1), *([cache3] * SEL_PAGES), qbd, ckcv, new_sel, win, win_new, gates)


def _diff_sample_kernel(tn, lam_init, n_steps, pt_ref, *refs):
    pages = refs[:DIFF_PAGES]
    qbd_ref, new_ref, lqk_ref, gsub_ref, o_ref, m_ref, l_ref, acc_ref = refs[DIFF_PAGES:]
    j = pl.program_id(1)
    page_rows = pages[0].shape[0]
    past = n_steps * DIFF_PAGES * page_rows
    rows = 2 * DIFF_HEADS * tn
    ridx = lax.broadcasted_iota(jnp.int32, (rows, 1), 0)
    qpos = past + ridx % tn
    slope = _pow2_neg(2 * (ridx // (2 * tn) + 1))
    qbd = qbd_ref[0]

    @pl.when(j == 0)
    def _():
        m_ref[...] = jnp.full(m_ref.shape, NEG, F32)
        l_ref[...] = jnp.zeros(l_ref.shape, F32)
        acc_ref[...] = jnp.zeros(acc_ref.shape, F32)

    def attend(ks, vs, kpos, extra_valid):
        dist = qpos - kpos
        valid = (dist >= 0) & extra_valid
        sc = jnp.concatenate([_dot_nt(qbd, k) for k in ks], axis=1)
        sc = jnp.where(valid, sc - slope * dist.astype(F32), NEG)
        m_old = m_ref[...]
        m_new = jnp.maximum(m_old, jnp.max(sc, axis=-1, keepdims=True))
        pr = jnp.where(valid, jnp.exp(sc - m_new), 0.0)
        alpha = jnp.exp(m_old - m_new)
        m_ref[...] = m_new
        l_ref[...] = alpha * l_ref[...] + jnp.sum(pr, axis=-1, keepdims=True)
        acc = alpha * acc_ref[...]
        n = ks[0].shape[0]
        for i, v in enumerate(vs):
            acc = acc + _dot(pr[:, i * n:(i + 1) * n].astype(BF16), v)
        acc_ref[...] = acc

    span = DIFF_PAGES * page_rows
    kpos = j * span + lax.broadcasted_iota(jnp.int32, (1, span), 1)
    attend([pg[:, :DQ_W].astype(BF16) for pg in pages],
           [pg[:, DQ_W:].astype(BF16) for pg in pages], kpos, True)

    @pl.when(j == n_steps - 1)
    def _():
        lane = lax.broadcasted_iota(jnp.int32, (1, LANES), 1)
        zpad = jnp.zeros((LANES - tn, DQ_W), F32)
        newk = jnp.concatenate([new_ref[0, :, :DQ_W], zpad], axis=0).astype(BF16)
        newv = jnp.concatenate([new_ref[0, :, DQ_W:], zpad], axis=0).astype(BF16)
        attend([newk], [newv], past + lane, lane < tn)
        o = _head_diag(acc_ref[...] / jnp.maximum(l_ref[...], 1e-30), DIFF_HEADS, 2 * tn, DIFF_VD)
        lam = _diff_lambda(lqk_ref, lam_init)
        for hd_i in range(DIFF_HEADS):
            o1 = o[hd_i * 2 * tn:hd_i * 2 * tn + tn]
            o2 = o[hd_i * 2 * tn + tn:(hd_i + 1) * 2 * tn]
            od = _rms(o1 - lam * o2, gsub_ref[...]) * (1.0 - lam_init)
            o_ref[0, :, hd_i * DIFF_VD:(hd_i + 1) * DIFF_VD] = od


def _diff_sample(page_table, cache3, qbd, new_diff, lqk, gsub, lam_init):
    bd, pps = page_table.shape
    page_rows, width = cache3.shape[1:]
    tn = new_diff.shape[1]
    rows = 2 * DIFF_HEADS * tn
    n_steps = pps // DIFF_PAGES
    per_b = lambda shape: pl.BlockSpec((1,) + shape, lambda b, j, pt: (b,) + (0,) * len(shape))
    in_specs = _page_specs(DIFF_PAGES, page_rows, width, 0, pps) + [
        per_b((rows, DQ_W)), per_b((tn, width)),
        pl.BlockSpec(lqk.shape, lambda b, j, pt: (0, 0)),
        pl.BlockSpec(gsub.shape, lambda b, j, pt: (0, 0))]
    grid_spec = pltpu.PrefetchScalarGridSpec(
        num_scalar_prefetch=1, grid=(bd, n_steps), in_specs=in_specs,
        out_specs=per_b((tn, DV_W)),
        scratch_shapes=[pltpu.VMEM((rows, 1), F32), pltpu.VMEM((rows, 1), F32),
                        pltpu.VMEM((rows, DV_W), F32)])
    return pl.pallas_call(
        functools.partial(_diff_sample_kernel, tn, lam_init, n_steps), grid_spec=grid_spec,
        out_shape=jax.ShapeDtypeStruct((bd, tn, DV_W), F32),
        compiler_params=_params(("arbitrary", "arbitrary")), name="diff_sample",
    )(page_table.reshape(-1), *([cache3] * DIFF_PAGES), qbd, new_diff, lqk, gsub)


def _block_diag_rows(q_sep, n_seq, tn):
    h = q_sep.shape[0]
    q = q_sep.reshape(h, n_seq, tn, HD).transpose(1, 0, 2, 3)
    eye = jnp.eye(h, dtype=q.dtype)
    return (q[:, :, :, None, :] * eye[None, :, None, :, None]).reshape(n_seq, h * tn, h * HD)


def _layer_weights(l, g_ffn1_pre, ffn1_w1, ffn1_w3, ffn1_w2, g_ffn1_post, g_mix_pre, w_in, cmp_w1,
                   cmp_b1, cmp_w2, lambda_qk, g_diff_subln, w_up_nsa, w_up_diff, w_out, g_mix_post,
                   g_ffn2_pre, ffn2_w1, ffn2_w3, ffn2_w2, g_ffn2_post):
    row = lambda g: g[l][None, :]
    b16 = lambda w: w[l].astype(BF16)
    wi = w_in[l]
    o_ng = NQ_W + NKV_W
    o_dq = o_ng + NG_W
    o_mg = o_dq + 2 * DQ_W + DV_W
    wp = jnp.concatenate(
        [wi[:, :o_ng], wi[:, o_dq:o_mg], wi[:, o_ng:o_dq],
         jnp.zeros((wi.shape[0], LANES - NG_W), wi.dtype)], axis=1).astype(BF16)
    w1c = cmp_w1[l]
    eye = jnp.eye(NSA_KV_HEADS, dtype=w1c.dtype)
    bd1 = (w1c[:, :, None, :, None, :] * eye[None, None, :, None, :, None]).reshape(
        2, NSA_BLOCK, NSA_KV_HEADS * HD, NSA_KV_HEADS * HD).astype(BF16)
    bd2 = (cmp_w2[l][:, None, :, None, :] * eye[None, :, None, :, None]).reshape(
        2, NSA_KV_HEADS * HD, NSA_KV_HEADS * HD).astype(BF16)
    return dict(
        ffn1=(row(g_ffn1_pre), b16(ffn1_w1), b16(ffn1_w3), b16(ffn1_w2), row(g_ffn1_post)),
        ffn2=(row(g_ffn2_pre), b16(ffn2_w1), b16(ffn2_w3), b16(ffn2_w2), row(g_ffn2_post)),
        gmix=row(g_mix_pre), wp=wp, wmg=wi[:, o_mg:].astype(BF16),
        w1r=w1c.reshape(2, NSA_BLOCK * HD, HD).astype(BF16), b1=cmp_b1[l][:, None, :],
        w2c=cmp_w2[l].astype(BF16), bd1=bd1, bd2=bd2,
        b1t=jnp.tile(cmp_b1[l], (1, NSA_KV_HEADS))[:, None, :],
        lqk=lambda_qk[l], gsub=row(g_diff_subln), wun=b16(w_up_nsa), wud=b16(w_up_diff),
        wout=b16(w_out), gmixpost=row(g_mix_post))


def _prompt_layer(x, W, lam_init):
    bsz, t, d = x.shape
    n = bsz * t
    (x1, q_sep, selwin, cmp_sep, nsa_new, win_new, diff_new, dq, dk, dv, ng) = _ffn_proj(
        x.reshape(n, d), *W["ffn1"], W["gmix"], W["wp"], tm=512)
    xc = cmp_sep.reshape(2 * NSA_KV_HEADS, n // NSA_BLOCK, NSA_BLOCK * HD)
    ckcv = _compress_prompt(xc, W["w1r"], W["b1"], W["w2c"])
    o_nsa = _nsa_prompt(q_sep, ckcv, selwin, ng, bsz, t)
    o_diff = _diff_prompt(dq, dk, dv, W["lqk"], W["gsub"], lam_init, bsz, t)
    y = _merge_ffn(x1, o_nsa, o_diff, W["gmix"], W["wmg"], W["wun"], W["wud"], W["wout"],
                   W["gmixpost"], *W["ffn2"], tm=512)
    w_rows = min(NSA_WINDOW, t)
    new_nsa = nsa_new.reshape(bsz, t, 4, NSA_KV_HEADS, HD)
    new_diff = diff_new.reshape(bsz, t, 2, DIFF_HEADS, 2 * HD)
    new_win = win_new.reshape(bsz, t, 2, NSA_KV_HEADS, HD)[:, t - w_rows:]
    return y.reshape(bsz, t, d), (new_nsa, new_diff, new_win)


def _sample_layer(x, c_nsa, c_diff, win, page_table, W, lam_init):
    bd, tn, d = x.shape
    n = bd * tn
    n_pool, page_rows = c_nsa.shape[:2]
    past = page_table.shape[1] * page_rows
    nb = -(-(past + tn) // NSA_BLOCK)
    (x1, q_sep, _, _, nsa_new, win_new, diff_new, dq, _, _, ng) = _ffn_proj(
        x.reshape(n, d), *W["ffn1"], W["gmix"], W["wp"], tm=n)
    nsa3 = c_nsa.reshape(n_pool, page_rows, 4 * NSA_KV_HEADS * HD)
    diff3 = c_diff.reshape(n_pool, page_rows, DQ_W + DV_W)
    nsa_new3 = nsa_new.reshape(bd, tn, 4 * NSA_KV_HEADS * HD)
    ckcv = _compress_sample(page_table, nsa3, nsa_new3[:, :, :2 * LANES], W["bd1"], W["b1t"], W["bd2"])
    qn = q_sep.reshape(NSA_KV_HEADS, NSA_GROUP, bd, tn, HD).transpose(2, 0, 1, 3, 4)
    eye = jnp.eye(NSA_KV_HEADS, dtype=qn.dtype)
    qbd_n = (qn[:, :, :, :, None, :] * eye[None, :, None, None, :, None]).reshape(
        bd, NSA_HEADS * tn, NSA_KV_HEADS * HD)
    gates = ng[:, :NG_W].reshape(bd, tn, NSA_HEADS, 3).transpose(0, 2, 1, 3).reshape(bd, NSA_HEADS * tn, 3)
    gates = jnp.pad(gates, ((0, 0), (0, 0), (0, LANES - 3)))
    win3 = win.reshape(bd, win.shape[1], 2 * NSA_KV_HEADS * HD)
    o_nsa = _nsa_sample(page_table, nsa3, qbd_n, ckcv, nsa_new3[:, :, 2 * LANES:], win3,
                        win_new.reshape(bd, tn, 2 * NSA_KV_HEADS * HD), gates, nb)
    qbd_d = _block_diag_rows(dq, bd, tn)
    o_diff = _diff_sample(page_table, diff3, qbd_d, diff_new.reshape(bd, tn, DQ_W + DV_W),
                          W["lqk"], W["gsub"], lam_init)
    y = _merge_ffn(x1, o_nsa.reshape(n, NQ_W), o_diff.reshape(n, DV_W), W["gmix"], W["wmg"],
                   W["wun"], W["wud"], W["wout"], W["gmixpost"], *W["ffn2"], tm=n)
    new_nsa = nsa_new.reshape(bd, tn, 4, NSA_KV_HEADS, HD)
    new_diff = diff_new.reshape(bd, tn, 2, DIFF_HEADS, 2 * HD)
    new_win = jnp.concatenate([win, win_new.reshape(bd, tn, 2, NSA_KV_HEADS, HD)], axis=1)[:, tn:]
    return y.reshape(bd, tn, d), (new_nsa, new_diff, new_win)


def kernel(x_prompt, x_sample, cache_nsa_kv, cache_diff_kv, state_win_kv, page_table, g_ffn1_pre, ffn1_w1, ffn1_w3, ffn1_w2, g_ffn1_post, g_mix_pre, w_in, cmp_w1, cmp_b1, cmp_w2, lambda_qk, g_diff_subln, w_up_nsa, w_up_diff, w_out, g_mix_post, g_ffn2_pre, ffn2_w1, ffn2_w3, ffn2_w2, g_ffn2_post):
    depth = w_in.shape[0]
    y_p, y_s = x_prompt, x_sample
    outs = [[] for _ in range(6)]
    for l in range(depth):
        lam_init = 0.8 - 0.6 * math.exp(-0.3 * l)
        W = _layer_weights(l, g_ffn1_pre, ffn1_w1, ffn1_w3, ffn1_w2, g_ffn1_post, g_mix_pre, w_in,
                           cmp_w1, cmp_b1, cmp_w2, lambda_qk, g_diff_subln, w_up_nsa, w_up_diff,
                           w_out, g_mix_post, g_ffn2_pre, ffn2_w1, ffn2_w3, ffn2_w2, g_ffn2_post)
        y_p, (a_p, b_p, c_p) = _prompt_layer(y_p, W, lam_init)
        y_s, (a_s, b_s, c_s) = _sample_layer(y_s, cache_nsa_kv[l], cache_diff_kv[l], state_win_kv[l],
                                             page_table, W, lam_init)
        for lst, val in zip(outs, (a_p, a_s, b_p, b_s, c_p, c_s)):
            lst.append(val)
    return (y_p, y_s) + tuple(jnp.stack(o) for o in outs)
```

```python
import functools
import math

import jax
import jax.numpy as jnp
from jax import lax
from jax.experimental import pallas as pl
from jax.experimental.pallas import tpu as pltpu

F32 = jnp.float32
BF16 = jnp.bfloat16

HD = 64
NSA_HEADS = 8
NSA_KV_HEADS = 2
NSA_GROUP = NSA_HEADS // NSA_KV_HEADS
NSA_BLOCK = 64
NSA_TOPN = 16
NSA_WINDOW = 512
DIFF_HEADS = 4
DIFF_VD = 2 * HD
EPS = 1e-6
FORCE_SCORE = 1e4
NEG = -1e30
LOG2E = 1.0 / math.log(2.0)
Q_SCALE = HD ** -0.5 * LOG2E
LANES = 128
SUBLANES = 8
FF_CHUNK = 256
VMEM_LIMIT = 56 * 1024 * 1024

NQ_W = NSA_HEADS * HD
NKV_W = 6 * NSA_KV_HEADS * HD
NSA_NEW_W = 4 * NSA_KV_HEADS * HD
CMP_W = 2 * NSA_KV_HEADS * HD
WIN_W = 2 * NSA_KV_HEADS * HD
NG_W = 3 * NSA_HEADS
DQ_W = DIFF_HEADS * 2 * HD
DV_W = DIFF_HEADS * DIFF_VD
DIFF_ROWS = 2 * DIFF_HEADS


def _nsa_slope2(head):
    return 2.0 ** (-8.0 * (head + 1) / NSA_HEADS) * LOG2E


def _dot(a, b):
    return jnp.dot(a, b, preferred_element_type=F32)


def _dot_nt(a, b):
    return lax.dot_general(a, b, (((1,), (1,)), ((), ())), preferred_element_type=F32)


def _rms(x, g):
    return x * lax.rsqrt(jnp.mean(x * x, axis=-1, keepdims=True) + EPS) * g


def _gelu_tanh(x):
    c = math.sqrt(2.0 / math.pi)
    return 0.5 * x * (1.0 + jnp.tanh(c * (x + 0.044715 * (x * x * x))))


def _const_spec(shape):
    nd = len(shape)
    return pl.BlockSpec(shape, lambda *_: (0,) * nd, pipeline_mode=pl.Buffered(1))


def _params(sem):
    return pltpu.CompilerParams(dimension_semantics=sem, vmem_limit_bytes=VMEM_LIMIT)


def _softmax2(s, valid):
    s = jnp.where(valid, s, NEG)
    p = jnp.where(valid, jnp.exp2(s - jnp.max(s, axis=-1, keepdims=True)), 0.0)
    return p / jnp.maximum(jnp.sum(p, axis=-1, keepdims=True), 1e-30)


def _ffn_half_step(x, g_pre, w1_ref, w3_ref, w2_ref, g_post):
    xn = _rms(x, g_pre).astype(BF16)
    d_ff = w1_ref.shape[1]
    acc = jnp.zeros(x.shape, F32)
    for c in range(d_ff // FF_CHUNK):
        cs = slice(c * FF_CHUNK, (c + 1) * FF_CHUNK)
        h1 = _dot(xn, w1_ref[:, cs])
        h3 = _dot(xn, w3_ref[:, cs])
        gate = (h1 * jax.nn.sigmoid(h1) * h3).astype(BF16)
        acc = acc + _dot(gate, w2_ref[cs, :])
    return x + 0.5 * _rms(acc, g_post)


def _ffn_proj_kernel(transposed, x_ref, gpre_ref, w1_ref, w3_ref, w2_ref, gpost_ref, gmix_ref,
                     wp_ref, wt_ref, x1_ref, q_ref, cmp_ref, nsa_new_ref, win_new_ref, selwin_ref,
                     diff_new_ref, dq_ref, dk_ref, dv_ref, ng_ref):
    tm = x_ref.shape[0]
    x1 = _ffn_half_step(x_ref[...], gpre_ref[...], w1_ref, w3_ref, w2_ref, gpost_ref[...])
    x1_ref[...] = x1
    h = _rms(x1, gmix_ref[...]).astype(BF16)
    o = 0
    nq = _dot(h, wp_ref[:, o:o + NQ_W]); o += NQ_W
    for i in range(NSA_HEADS):
        q_ref[i] = (nq[:, i * HD:(i + 1) * HD] * Q_SCALE).astype(BF16)
    if transposed:
        cmp = _dot(h, wp_ref[:, o:o + CMP_W])
        nkv_t = _dot_nt(wt_ref[...], h)
        nsa_new_ref[0] = nkv_t[:NSA_NEW_W]
        win_new_ref[0] = nkv_t[NSA_NEW_W:]
        for i in range(4 * NSA_KV_HEADS):
            selwin_ref[i] = nkv_t[CMP_W + i * HD:CMP_W + (i + 1) * HD].astype(BF16)
    else:
        nkv = _dot(h, wp_ref[:, o:o + NKV_W])
        cmp = nkv[:, :CMP_W]
        nsa_new_ref[...] = nkv[:, :NSA_NEW_W]
        win_new_ref[...] = nkv[:, NSA_NEW_W:]
        selwin_ref[...] = jnp.zeros(selwin_ref.shape, BF16)
    o += NKV_W
    for i in range(2 * NSA_KV_HEADS):
        cmp_ref[i] = cmp[:, i * HD:(i + 1) * HD].astype(BF16)
    dq = _dot(h, wp_ref[:, o:o + DQ_W]); o += DQ_W
    for i in range(2 * DIFF_HEADS):
        dq_ref[i] = (dq[:, i * HD:(i + 1) * HD] * Q_SCALE).astype(BF16)
    dk = _dot(h, wp_ref[:, o:o + DQ_W]); o += DQ_W
    dv = _dot(h, wp_ref[:, o:o + DV_W]); o += DV_W
    for i in range(DIFF_HEADS):
        diff_new_ref[pl.ds(i, tm, stride=DIFF_ROWS), :] = dk[:, i * 2 * HD:(i + 1) * 2 * HD]
        diff_new_ref[pl.ds(DIFF_HEADS + i, tm, stride=DIFF_ROWS), :] = dv[:, i * DIFF_VD:(i + 1) * DIFF_VD]
        dv_ref[i] = dv[:, i * DIFF_VD:(i + 1) * DIFF_VD].astype(BF16)
    for i in range(2 * DIFF_HEADS):
        dk_ref[i] = dk[:, i * HD:(i + 1) * HD].astype(BF16)
    ng_ref[...] = jax.nn.sigmoid(_dot(h, wp_ref[:, o:o + LANES]))


def _ffn_proj(x, gpre, w1, w3, w2, gpost, gmix, wp, wt, tm, seq_len=None):
    n, d = x.shape
    d_ff = w1.shape[1]
    row = lambda w: pl.BlockSpec((tm, w), lambda i: (i, 0))
    sep = lambda k, w: pl.BlockSpec((k, tm, w), lambda i: (0, i, 0))
    transposed = seq_len is not None
    if transposed:
        tiles = seq_len // tm
        nseq = n // seq_len
        fmaj = lambda w: pl.BlockSpec((1, w, tm), lambda i: (i // tiles, 0, i % tiles))
        nsa_shape, nsa_spec = jax.ShapeDtypeStruct((nseq, NSA_NEW_W, seq_len), F32), fmaj(NSA_NEW_W)
        win_shape, win_spec = jax.ShapeDtypeStruct((nseq, WIN_W, seq_len), F32), fmaj(WIN_W)
        sw_shape = jax.ShapeDtypeStruct((4 * NSA_KV_HEADS, HD, n), BF16)
        sw_spec = pl.BlockSpec((4 * NSA_KV_HEADS, HD, tm), lambda i: (0, 0, i))
    else:
        nsa_shape, nsa_spec = jax.ShapeDtypeStruct((n, NSA_NEW_W), F32), row(NSA_NEW_W)
        win_shape, win_spec = jax.ShapeDtypeStruct((n, WIN_W), F32), row(WIN_W)
        sw_shape = jax.ShapeDtypeStruct((2 * SUBLANES, LANES), BF16)
        sw_spec = pl.BlockSpec((2 * SUBLANES, LANES), lambda i: (0, 0))
    out_shape = (
        jax.ShapeDtypeStruct((n, d), F32),
        jax.ShapeDtypeStruct((NSA_HEADS, n, HD), BF16),
        jax.ShapeDtypeStruct((2 * NSA_KV_HEADS, n, HD), BF16),
        nsa_shape,
        win_shape,
        sw_shape,
        jax.ShapeDtypeStruct((n * DIFF_ROWS, LANES), F32),
        jax.ShapeDtypeStruct((2 * DIFF_HEADS, n, HD), BF16),
        jax.ShapeDtypeStruct((2 * DIFF_HEADS, n, HD), BF16),
        jax.ShapeDtypeStruct((DIFF_HEADS, n, DIFF_VD), BF16),
        jax.ShapeDtypeStruct((n, LANES), F32),
    )
    out_specs = (row(d), sep(NSA_HEADS, HD), sep(2 * NSA_KV_HEADS, HD), nsa_spec, win_spec, sw_spec,
                 pl.BlockSpec((tm * DIFF_ROWS, LANES), lambda i: (i, 0)),
                 sep(2 * DIFF_HEADS, HD), sep(2 * DIFF_HEADS, HD), sep(DIFF_HEADS, DIFF_VD), row(LANES))
    in_specs = [row(d), _const_spec((1, d)), _const_spec((d, d_ff)), _const_spec((d, d_ff)),
                _const_spec((d_ff, d)), _const_spec((1, d)), _const_spec((1, d)),
                _const_spec(wp.shape), _const_spec(wt.shape)]
    return pl.pallas_call(
        functools.partial(_ffn_proj_kernel, transposed), grid=(n // tm,), in_specs=in_specs,
        out_specs=out_specs, out_shape=out_shape, compiler_params=_params(("parallel",)),
        name="ffn_proj",
    )(x, gpre, w1, w3, w2, gpost, gmix, wp, wt)


def _compress_kernel(x_ref, w1_ref, b1_ref, w2_ref, o_ref):
    hid = _gelu_tanh(_dot(x_ref[0], w1_ref[0]) + b1_ref[0])
    o_ref[0] = _dot(hid.astype(BF16), w2_ref[0])


def _compress_prompt(xc, w1r, b1, w2):
    k, nb, kd = xc.shape
    return pl.pallas_call(
        _compress_kernel, grid=(k,),
        in_specs=[pl.BlockSpec((1, nb, kd), lambda i: (i, 0, 0)),
                  pl.BlockSpec((1, kd, HD), lambda i: (i // NSA_KV_HEADS, 0, 0)),
                  pl.BlockSpec((1, 1, HD), lambda i: (i // NSA_KV_HEADS, 0, 0)),
                  pl.BlockSpec((1, HD, HD), lambda i: (i // NSA_KV_HEADS, 0, 0))],
        out_specs=pl.BlockSpec((1, nb, HD), lambda i: (i, 0, 0)),
        out_shape=jax.ShapeDtypeStruct((k, nb, HD), F32),
        compiler_params=_params(("parallel",)), name="nsa_compress_prompt",
    )(xc, w1r, b1, w2)


NSA_TQ = 128
NSA_TK = 512


def _topn_mask_t(imp_t, nb):
    nt = imp_t.shape[1]
    rank = jnp.zeros((nb, nt), F32)
    vals = imp_t[:nb]
    jidx = lax.broadcasted_iota(jnp.int32, (nb, nt), 0)
    for i in range(nb):
        vi = jnp.broadcast_to(imp_t[i:i + 1, :], (nb, nt))
        ahead = (vi > vals) | ((vi == vals) & (jidx > i))
        rank = rank + jnp.where(ahead, 1.0, 0.0)
    return jnp.where((rank < NSA_TOPN) & (vals >= 0.0), 1.0, 0.0)


def _nsa_prompt_kernel(q_ref, ckcv_ref, sw_ref, ng_ref, o_ref):
    qi = pl.program_id(1)
    s0 = qi * NSA_TQ
    nb = ckcv_ref.shape[1]
    rows = NSA_GROUP * NSA_TQ
    ridx = lax.broadcasted_iota(jnp.int32, (rows, 1), 0)
    tok = ridx % NSA_TQ
    gidx = ridx // NSA_TQ
    ng = ng_ref[...]
    lane_k = lax.broadcasted_iota(jnp.int32, (1, NSA_TK), 1)
    for kvh in range(NSA_KV_HEADS):
        slope = jnp.zeros((rows, 1), F32)
        for g in range(NSA_GROUP):
            slope = jnp.where(gidx == g, _nsa_slope2(kvh * NSA_GROUP + g), slope)
        q = q_ref[kvh * NSA_GROUP:(kvh + 1) * NSA_GROUP].reshape(rows, HD)

        ck = ckcv_ref[kvh].astype(BF16)
        cv = ckcv_ref[NSA_KV_HEADS + kvh].astype(BF16)
        blk = lax.broadcasted_iota(jnp.int32, (1, nb), 1)
        end_c = (blk + 1) * NSA_BLOCK - 1 - s0
        p = _softmax2(_dot_nt(q, ck) + slope * end_c.astype(F32), tok >= end_c)
        o_c = _dot(p.astype(BF16), cv)

        imp = p[0:NSA_TQ]
        for g in range(1, NSA_GROUP):
            imp = imp + p[g * NSA_TQ:(g + 1) * NSA_TQ]
        cur = (s0 + lax.broadcasted_iota(jnp.int32, (NSA_TQ, 1), 0)) // NSA_BLOCK
        forced = (blk == cur) | (blk == 0)
        imp = jnp.where(forced, FORCE_SCORE, jnp.where(blk <= cur, imp, -1.0))
        pad = jnp.full((NSA_TQ, LANES - nb), -1.0, F32)
        imp_t = jnp.concatenate([imp, pad], axis=1).T
        sel_t = _topn_mask_t(imp_t, nb)
        sel_t = jnp.concatenate([sel_t, jnp.zeros((LANES - nb, NSA_TQ), F32)], axis=0)
        sel = sel_t.T.astype(BF16)

        bias0 = slope * lane_k.astype(F32)

        def sel_tile(kt, carry, diagonal):
            m, l, acc = carry
            k0 = pl.multiple_of(kt * NSA_TK, NSA_TK)
            k_t = sw_ref[kvh, :, pl.ds(k0, NSA_TK)]
            v_t = sw_ref[NSA_KV_HEADS + kvh, :, pl.ds(k0, NSA_TK)]
            expand = (lax.broadcasted_iota(jnp.int32, (LANES, NSA_TK), 0)
                      == (k0 + lane_k) // NSA_BLOCK).astype(BF16)
            hit = _dot(sel, expand)
            valid = jnp.concatenate([hit] * NSA_GROUP, axis=0) > 0.5
            if diagonal:
                valid = valid & (k0 - s0 + lane_k <= tok)
            t = jnp.where(valid, _dot(q, k_t) + bias0, NEG)
            shift = slope * (k0 - s0).astype(F32)
            m_new = jnp.maximum(m, jnp.max(t, axis=-1, keepdims=True) + shift)
            pr = jnp.where(valid, jnp.exp2(t - (m_new - shift)), 0.0)
            alpha = jnp.exp2(m - m_new)
            l = alpha * l + jnp.sum(pr, axis=-1, keepdims=True)
            acc = alpha * acc + _dot_nt(pr.astype(BF16), v_t)
            return m_new, l, acc

        n_full = s0 // NSA_TK
        init = (jnp.full((rows, 1), NEG, F32), jnp.zeros((rows, 1), F32), jnp.zeros((rows, HD), F32))
        carry = lax.fori_loop(0, n_full, functools.partial(sel_tile, diagonal=False), init)
        _, l_s, acc_s = sel_tile(n_full, carry, True)
        o_s = acc_s / jnp.maximum(l_s, 1e-30)

        span = NSA_WINDOW + NSA_TQ
        w0 = pl.multiple_of(jnp.maximum(s0 - NSA_WINDOW, 0), NSA_TQ)
        kw_t = sw_ref[2 * NSA_KV_HEADS + kvh, :, pl.ds(w0, span)]
        vw_t = sw_ref[3 * NSA_KV_HEADS + kvh, :, pl.ds(w0, span)]
        krel = w0 - s0 + lax.broadcasted_iota(jnp.int32, (1, span), 1)
        dist = tok - krel
        pr = _softmax2(_dot(q, kw_t) + slope * krel.astype(F32), (dist >= 0) & (dist < NSA_WINDOW))
        o_w = _dot_nt(pr.astype(BF16), vw_t)

        for g in range(NSA_GROUP):
            hd_i = kvh * NSA_GROUP + g
            r = slice(g * NSA_TQ, (g + 1) * NSA_TQ)
            out = (ng[:, 3 * hd_i:3 * hd_i + 1] * o_c[r] + ng[:, 3 * hd_i + 1:3 * hd_i + 2] * o_s[r]
                   + ng[:, 3 * hd_i + 2:3 * hd_i + 3] * o_w[r])
            o_ref[:, hd_i * HD:(hd_i + 1) * HD] = out.astype(o_ref.dtype)


def _nsa_prompt(q_sep, ckcv, selwin_t, ng, bsz, t):
    nq = t // NSA_TQ
    nb = t // NSA_BLOCK
    n = bsz * t
    return pl.pallas_call(
        _nsa_prompt_kernel, grid=(bsz, nq),
        in_specs=[pl.BlockSpec((NSA_HEADS, NSA_TQ, HD), lambda b, i: (0, b * nq + i, 0)),
                  pl.BlockSpec((2 * NSA_KV_HEADS, nb, HD), lambda b, i: (0, b, 0)),
                  pl.BlockSpec((4 * NSA_KV_HEADS, HD, t), lambda b, i: (0, 0, b)),
                  pl.BlockSpec((NSA_TQ, LANES), lambda b, i: (b * nq + i, 0))],
        out_specs=pl.BlockSpec((NSA_TQ, NQ_W), lambda b, i: (b * nq + i, 0)),
        out_shape=jax.ShapeDtypeStruct((n, NQ_W), BF16),
        compiler_params=_params(("parallel", "parallel")), name="nsa_prompt",
    )(q_sep, ckcv, selwin_t, ng)


DIFF_T = 512


def _diff_lambda(lqk_ref, lam_init):
    l = lqk_ref[...]
    a = jnp.sum(l[0:1] * l[1:2], axis=-1, keepdims=True)
    b = jnp.sum(l[2:3] * l[3:4], axis=-1, keepdims=True)
    return jnp.exp(a) - jnp.exp(b) + lam_init


def _pow2_neg(n_vec):
    return lax.bitcast_convert_type((127 - n_vec) << 23, F32)


def _diff_prompt_kernel(lam_init, q_ref, k_ref, v_ref, lqk_ref, gsub_ref, o_ref):
    h = pl.program_id(1)
    qi = pl.program_id(2)
    s0 = qi * DIFF_T
    slope = _pow2_neg(jnp.full((1, 1), 2 * (h + 1), jnp.int32)) * LOG2E
    lane = lax.broadcasted_iota(jnp.int32, (1, DIFF_T), 1)
    causal = (lax.broadcasted_iota(jnp.int32, (DIFF_T, DIFF_T), 0)
              >= lax.broadcasted_iota(jnp.int32, (DIFF_T, DIFF_T), 1))
    q1 = q_ref[0]
    q2 = q_ref[1]

    def tile(kt, carry, diagonal):
        m1, l1, m2, l2, acc = carry
        k0 = pl.multiple_of(kt * DIFF_T, DIFF_T)
        kbias = slope * (k0 - s0 + lane).astype(F32)
        v = v_ref[0, pl.ds(k0, DIFF_T), :]

        def one(q, k, m, l):
            sc = _dot_nt(q, k) + kbias
            if diagonal:
                sc = jnp.where(causal, sc, NEG)
            m_new = jnp.maximum(m, jnp.max(sc, axis=-1, keepdims=True))
            pr = jnp.exp2(sc - m_new)
            alpha = jnp.exp2(m - m_new)
            return m_new, alpha * l + jnp.sum(pr, axis=-1, keepdims=True), alpha, pr

        m1, l1, a1, p1 = one(q1, k_ref[0, pl.ds(k0, DIFF_T), :], m1, l1)
        m2, l2, a2, p2 = one(q2, k_ref[1, pl.ds(k0, DIFF_T), :], m2, l2)
        pv = _dot(jnp.concatenate([p1, p2], axis=0).astype(BF16), v)
        acc = jnp.concatenate([a1, a2], axis=0) * acc + pv
        return m1, l1, m2, l2, acc

    col = lambda val: jnp.full((DIFF_T, 1), val, F32)
    init = (col(NEG), col(0.0), col(NEG), col(0.0), jnp.zeros((2 * DIFF_T, DIFF_VD), F32))
    carry = lax.fori_loop(0, qi, functools.partial(tile, diagonal=False), init)
    _, l1, _, l2, acc = tile(qi, carry, True)
    lam = _diff_lambda(lqk_ref, lam_init)
    o = acc[:DIFF_T] / l1 - lam * (acc[DIFF_T:] / l2)
    o_ref[...] = (_rms(o, gsub_ref[...]) * (1.0 - lam_init)).astype(o_ref.dtype)


def _diff_prompt(dq, dk, dv, lqk, gsub, lam_init, bsz, t):
    nq = t // DIFF_T
    n = bsz * t
    return pl.pallas_call(
        functools.partial(_diff_prompt_kernel, lam_init), grid=(bsz, DIFF_HEADS, nq),
        in_specs=[pl.BlockSpec((2, DIFF_T, HD), lambda b, h, i: (h, b * nq + i, 0)),
                  pl.BlockSpec((2, t, HD), lambda b, h, i: (h, b, 0)),
                  pl.BlockSpec((1, t, DIFF_VD), lambda b, h, i: (h, b, 0)),
                  pl.BlockSpec(lqk.shape, lambda b, h, i: (0, 0)),
                  pl.BlockSpec(gsub.shape, lambda b, h, i: (0, 0))],
        out_specs=pl.BlockSpec((DIFF_T, DIFF_VD), lambda b, h, i: (b * nq + i, h)),
        out_shape=jax.ShapeDtypeStruct((n, DV_W), BF16),
        compiler_params=_params(("parallel", "parallel", "parallel")), name="diff_prompt",
    )(dq, dk, dv, lqk, gsub)


def _merge_ffn_kernel(x_ref, on_ref, od_ref, gmix_ref, wmg_ref, wun_ref, wud_ref, wout_ref,
                      gmixpost_ref, gpre_ref, w1_ref, w3_ref, w2_ref, gpost_ref, y_ref):
    x1 = x_ref[...]
    d = x1.shape[1]
    h = _rms(x1, gmix_ref[...]).astype(BF16)
    up_n = _dot(on_ref[...].astype(BF16), wun_ref[...])
    up_d = _dot(od_ref[...].astype(BF16), wud_ref[...])
    m = (jax.nn.sigmoid(_dot(h, wmg_ref[:, :d])) * up_n
         + jax.nn.sigmoid(_dot(h, wmg_ref[:, d:])) * up_d)
    out = _dot(m.astype(BF16), wout_ref[...])
    x2 = x1 + _rms(out, gmixpost_ref[...])
    y_ref[...] = _ffn_half_step(x2, gpre_ref[...], w1_ref, w3_ref, w2_ref, gpost_ref[...])


def _merge_ffn(x1, o_nsa, o_diff, gmix, wmg, wun, wud, wout, gmixpost, gpre, w1, w3, w2, gpost, tm):
    n, d = x1.shape
    d_ff = w1.shape[1]
    row = lambda w: pl.BlockSpec((tm, w), lambda i: (i, 0))
    in_specs = [row(d), row(NQ_W), row(DV_W), _const_spec((1, d)), _const_spec(wmg.shape),
                _const_spec(wun.shape), _const_spec(wud.shape), _const_spec(wout.shape),
                _const_spec((1, d)), _const_spec((1, d)), _const_spec((d, d_ff)),
                _const_spec((d, d_ff)), _const_spec((d_ff, d)), _const_spec((1, d))]
    return pl.pallas_call(
        _merge_ffn_kernel, grid=(n // tm,), in_specs=in_specs, out_specs=row(d),
        out_shape=jax.ShapeDtypeStruct((n, d), F32),
        compiler_params=_params(("parallel",)), name="merge_ffn",
    )(x1, o_nsa, o_diff, gmix, wmg, wun, wud, wout, gmixpost, gpre, w1, w3, w2, gpost)


CMP_PAGES = 16
SEL_PAGES = 16
DIFF_PAGES = 8
CK_ROWS = 384


def _page_specs(n_pages, block_rows, half, pages_per_seq):
    def make(i):
        def imap(b, j, pt):
            return (pt[b * pages_per_seq + j * n_pages + i], half, 0)
        return pl.BlockSpec((None, block_rows, LANES), imap)
    return [make(i) for i in range(n_pages)]


def _compress_sample_kernel(n_steps, slots, pt_ref, *refs):
    pages = refs[:CMP_PAGES]
    new_ref, wd_ref, b1_ref, w2_ref, o_ref, xs_ref = refs[CMP_PAGES:]
    j = pl.program_id(1)
    n_pages = n_steps * CMP_PAGES

    @pl.when(j == 0)
    def _():
        xs_ref[pl.ds(n_pages * CMP_W, CMP_W), :] = new_ref[0]
        xs_ref[pl.ds((n_pages + 1) * CMP_W, (slots - n_pages - 1) * CMP_W), :] = jnp.zeros(
            ((slots - n_pages - 1) * CMP_W, LANES), F32)

    for i, pg in enumerate(pages):
        r0 = pl.multiple_of((j * CMP_PAGES + i) * CMP_W, CMP_W)
        xs_ref[pl.ds(r0, CMP_W), :] = pg[...]

    @pl.when(j == n_steps - 1)
    def _():
        for c in range(2 * NSA_KV_HEADS):
            kind = c // NSA_KV_HEADS

            def body(d, acc):
                lhs = xs_ref[pl.ds(c * HD + d, slots, stride=CMP_W), :].astype(BF16)
                return acc + _dot(lhs, wd_ref[kind, d])

            acc = lax.fori_loop(0, HD, body, jnp.zeros((slots, LANES), F32))
            hid = _gelu_tanh(acc + b1_ref[kind]).astype(BF16)
            o_ref[0, c] = _dot(hid, w2_ref[kind])


def _compress_sample(page_table, cache_t, new_cmp_t, wd, b1t, bd2):
    bd, pps = page_table.shape
    slots = -(-(pps + 1) // SUBLANES) * SUBLANES
    n_steps = pps // CMP_PAGES
    in_specs = _page_specs(CMP_PAGES, CMP_W, 0, pps) + [
        pl.BlockSpec((1, CMP_W, LANES), lambda b, j, pt: (b, 0, 0)),
        pl.BlockSpec(wd.shape, lambda b, j, pt: (0, 0, 0, 0), pipeline_mode=pl.Buffered(1)),
        pl.BlockSpec(b1t.shape, lambda b, j, pt: (0, 0, 0), pipeline_mode=pl.Buffered(1)),
        pl.BlockSpec(bd2.shape, lambda b, j, pt: (0, 0, 0), pipeline_mode=pl.Buffered(1))]
    grid_spec = pltpu.PrefetchScalarGridSpec(
        num_scalar_prefetch=1, grid=(bd, n_steps), in_specs=in_specs,
        out_specs=pl.BlockSpec((1, 2 * NSA_KV_HEADS, slots, LANES), lambda b, j, pt: (b, 0, 0, 0)),
        scratch_shapes=[pltpu.VMEM((slots * CMP_W, LANES), F32)])
    return pl.pallas_call(
        functools.partial(_compress_sample_kernel, n_steps, slots), grid_spec=grid_spec,
        out_shape=jax.ShapeDtypeStruct((bd, 2 * NSA_KV_HEADS, slots, LANES), F32),
        compiler_params=_params(("arbitrary", "arbitrary")), name="nsa_compress_sample",
    )(page_table.reshape(-1), *([cache_t] * CMP_PAGES), new_cmp_t, wd, b1t, bd2)


def _head_diag(full, n_heads, rows_per_head, width):
    return jnp.concatenate(
        [full[h * rows_per_head:(h + 1) * rows_per_head, h * width:(h + 1) * width]
         for h in range(n_heads)], axis=0)


def _pad_rows(x, n):
    return jnp.concatenate([x, jnp.zeros((n - x.shape[0], x.shape[1]), x.dtype)], axis=0)


def _nsa_sample_kernel(tn, nb, n_steps, pt_ref, *refs):
    pages = refs[:SEL_PAGES]
    (qbd_ref, q_ref, ckcv_ref, new_ref, win_ref, wnew_ref, gate_ref, o_ref,
     sel_ref, oc_ref, m_ref, l_ref, acc_ref) = refs[SEL_PAGES:]
    j = pl.program_id(1)
    page_rows = pages[0].shape[1]
    past = n_steps * SEL_PAGES * page_rows
    rows = NSA_HEADS * tn
    rpk = NSA_GROUP * tn
    kvw = NSA_KV_HEADS * HD
    ridx = lax.broadcasted_iota(jnp.int32, (rows, 1), 0)
    tok = ridx % tn
    head = ridx // tn
    slope = jnp.zeros((rows, 1), F32)
    for hd_i in range(NSA_HEADS):
        slope = jnp.where(head == hd_i, _nsa_slope2(hd_i), slope)
    qbd = qbd_ref[0]

    @pl.when(j == 0)
    def _():
        blk = lax.broadcasted_iota(jnp.int32, (1, CK_ROWS), 1)
        end_c = (blk + 1) * NSA_BLOCK - 1 - past
        imps, ocs = [], []
        for kvh in range(NSA_KV_HEADS):
            r = slice(kvh * rpk, (kvh + 1) * rpk)
            ck = ckcv_ref[0, kvh].astype(BF16)
            cv = ckcv_ref[0, NSA_KV_HEADS + kvh].astype(BF16)
            p = _softmax2(_dot_nt(q_ref[0, r], ck) + slope[r] * end_c.astype(F32),
                          (tok[r] >= end_c) & (blk < nb))
            ocs.append(_dot(p.astype(BF16), cv))
            imp = p[0:tn]
            for g in range(1, NSA_GROUP):
                imp = imp + p[g * tn:(g + 1) * tn]
            imps.append(imp)
        oc_ref[...] = jnp.concatenate(ocs, axis=0)
        imp = jnp.concatenate(imps, axis=0)
        tq = past + lax.broadcasted_iota(jnp.int32, (NSA_KV_HEADS * tn, 1), 0) % tn
        cur = tq // NSA_BLOCK
        forced = (blk == cur) | (blk == 0)
        imp = jnp.where(forced, FORCE_SCORE, jnp.where(blk <= cur, imp, -1.0))
        imp = jnp.where(blk < nb, imp, -3.0)

        def pick(_, carry):
            vals, sel = carry
            top = jnp.max(vals, axis=-1, keepdims=True)
            first = jnp.min(jnp.where(vals == top, blk, CK_ROWS), axis=-1, keepdims=True)
            hit = blk == first
            sel = jnp.where(hit & (top >= 0.0), 1.0, sel)
            return jnp.where(hit, -2.0, vals), sel

        _, sel = lax.fori_loop(0, min(NSA_TOPN, nb), pick, (imp, jnp.zeros(imp.shape, F32)))
        sel_rows = jnp.concatenate(
            [sel[kvh * tn:(kvh + 1) * tn] for kvh in range(NSA_KV_HEADS) for _ in range(NSA_GROUP)],
            axis=0)
        for tile in range(CK_ROWS // LANES):
            sel_ref[tile] = sel_rows[:, tile * LANES:(tile + 1) * LANES]
        m_ref[...] = jnp.full(m_ref.shape, NEG, F32)
        l_ref[...] = jnp.zeros(l_ref.shape, F32)
        acc_ref[...] = jnp.zeros(acc_ref.shape, F32)

    def attend(scores, values, krel, blk_rel, sel_tile, extra_valid):
        n = scores.shape[1]
        expand = (lax.broadcasted_iota(jnp.int32, (LANES, n), 0) == blk_rel).astype(BF16)
        hit = _dot(sel_tile.astype(BF16), expand)
        valid = (hit > 0.5) & (krel <= tok) & extra_valid
        sc = jnp.where(valid, scores + slope * krel.astype(F32), NEG)
        m_old = m_ref[...]
        m_new = jnp.maximum(m_old, jnp.max(sc, axis=-1, keepdims=True))
        pr = jnp.where(valid, jnp.exp2(sc - m_new), 0.0)
        alpha = jnp.exp2(m_old - m_new)
        m_ref[...] = m_new
        l_ref[...] = alpha * l_ref[...] + jnp.sum(pr, axis=-1, keepdims=True)
        acc_ref[...] = alpha * acc_ref[...] + values(pr.astype(BF16))

    span = SEL_PAGES * page_rows
    blk0 = j * (span // NSA_BLOCK)
    tile = blk0 // LANES
    lane = lax.broadcasted_iota(jnp.int32, (1, span), 1)
    scores = jnp.concatenate([_dot(qbd, pg[:kvw, :].astype(BF16)) for pg in pages], axis=1)

    def page_values(pr):
        out = jnp.zeros((rows, kvw), F32)
        for i, pg in enumerate(pages):
            out = out + _dot_nt(pr[:, i * page_rows:(i + 1) * page_rows], pg[kvw:, :].astype(BF16))
        return out

    attend(scores, page_values, j * span - past + lane, blk0 - tile * LANES + lane // NSA_BLOCK,
           sel_ref[tile], True)

    @pl.when(j == n_steps - 1)
    def _():
        lane1 = lax.broadcasted_iota(jnp.int32, (1, LANES), 1)
        newk = _pad_rows(new_ref[0, :, :kvw], LANES).astype(BF16)
        newv = _pad_rows(new_ref[0, :, kvw:], LANES).astype(BF16)
        nblk0 = past // NSA_BLOCK
        ntile = nblk0 // LANES
        attend(_dot_nt(qbd, newk), lambda pr: _dot(pr, newv), lane1,
               nblk0 - ntile * LANES + lane1 // NSA_BLOCK, sel_ref[ntile], lane1 < tn)
        o_s = _head_diag(acc_ref[...] / jnp.maximum(l_ref[...], 1e-30), NSA_KV_HEADS, rpk, HD)

        wb = win_ref.shape[2]
        wnk = _pad_rows(wnew_ref[0, :, :kvw], LANES).astype(BF16)
        wnv = _pad_rows(wnew_ref[0, :, kvw:], LANES).astype(BF16)
        sc = jnp.concatenate([_dot(qbd, win_ref[0, :kvw, :].astype(BF16)), _dot_nt(qbd, wnk)], axis=1)
        lane_w = lax.broadcasted_iota(jnp.int32, (1, wb + LANES), 1)
        krel = lane_w - wb
        dist = tok - krel
        valid = (dist >= 0) & (dist < NSA_WINDOW) & (krel + past >= 0) & (lane_w < wb + tn)
        pr = _softmax2(sc + slope * krel.astype(F32), valid).astype(BF16)
        o_w = _head_diag(_dot_nt(pr[:, :wb], win_ref[0, kvw:, :].astype(BF16)) + _dot(pr[:, wb:], wnv),
                         NSA_KV_HEADS, rpk, HD)

        gt = gate_ref[0]
        out = gt[:, 0:1] * oc_ref[...] + gt[:, 1:2] * o_s + gt[:, 2:3] * o_w
        for hd_i in range(NSA_HEADS):
            o_ref[0, :, hd_i * HD:(hd_i + 1) * HD] = out[hd_i * tn:(hd_i + 1) * tn]


def _nsa_sample(page_table, cache_t, qbd, q_rows, ckcv, new_sel, win_t, win_new, gates, nb):
    bd, pps = page_table.shape
    tn = new_sel.shape[1]
    rows = NSA_HEADS * tn
    kvw = NSA_KV_HEADS * HD
    n_steps = pps // SEL_PAGES
    per_b = lambda shape: pl.BlockSpec((1,) + shape, lambda b, j, pt: (b,) + (0,) * len(shape))
    in_specs = _page_specs(SEL_PAGES, 2 * kvw, 1, pps) + [
        per_b((rows, kvw)), per_b((rows, HD)), per_b((2 * NSA_KV_HEADS, CK_ROWS, HD)),
        per_b((tn, 2 * kvw)), per_b(win_t.shape[1:]), per_b((tn, 2 * kvw)), per_b((rows, LANES))]
    grid_spec = pltpu.PrefetchScalarGridSpec(
        num_scalar_prefetch=1, grid=(bd, n_steps), in_specs=in_specs,
        out_specs=per_b((tn, NQ_W)),
        scratch_shapes=[pltpu.VMEM((CK_ROWS // LANES, rows, LANES), F32), pltpu.VMEM((rows, HD), F32),
                        pltpu.VMEM((rows, 1), F32), pltpu.VMEM((rows, 1), F32),
                        pltpu.VMEM((rows, kvw), F32)])
    return pl.pallas_call(
        functools.partial(_nsa_sample_kernel, tn, nb, n_steps), grid_spec=grid_spec,
        out_shape=jax.ShapeDtypeStruct((bd, tn, NQ_W), F32),
        compiler_params=_params(("arbitrary", "arbitrary")), name="nsa_sample",
    )(page_table.reshape(-1), *([cache_t] * SEL_PAGES), qbd, q_rows, ckcv, new_sel, win_t, win_new, gates)


def _diff_sample_kernel(tn, lam_init, n_steps, pt_ref, *refs):
    pages = refs[:DIFF_PAGES]
    qbd_ref, new_ref, lqk_ref, gsub_ref, o_ref, m_ref, l_ref, acc_ref = refs[DIFF_PAGES:]
    j = pl.program_id(1)
    page_rows = pages[0].shape[0] // DIFF_ROWS
    past = n_steps * DIFF_PAGES * page_rows
    rph = 2 * tn
    rows = DIFF_HEADS * rph
    ridx = lax.broadcasted_iota(jnp.int32, (rows, 1), 0)
    tok = ridx % tn
    slope = _pow2_neg(2 * (ridx // rph + 1)) * LOG2E

    @pl.when(j == 0)
    def _():
        m_ref[...] = jnp.full(m_ref.shape, NEG, F32)
        l_ref[...] = jnp.zeros(l_ref.shape, F32)
        acc_ref[...] = jnp.zeros(acc_ref.shape, F32)

    def attend(key_sets, krel, extra_valid):
        sc = jnp.concatenate(
            [jnp.concatenate([_dot_nt(qbd_ref[0, hd_i], ks(hd_i, 0).astype(BF16)) for ks in key_sets], axis=1)
             for hd_i in range(DIFF_HEADS)], axis=0)
        valid = (krel <= tok) & extra_valid
        sc = jnp.where(valid, sc + slope * krel.astype(F32), NEG)
        m_old = m_ref[...]
        m_new = jnp.maximum(m_old, jnp.max(sc, axis=-1, keepdims=True))
        pr = jnp.where(valid, jnp.exp2(sc - m_new), 0.0)
        alpha = jnp.exp2(m_old - m_new)
        m_ref[...] = m_new
        l_ref[...] = alpha * l_ref[...] + jnp.sum(pr, axis=-1, keepdims=True)
        pr = pr.astype(BF16)
        n = sc.shape[1] // len(key_sets)
        pvs = []
        for hd_i in range(DIFF_HEADS):
            pv = jnp.zeros((rph, DIFF_VD), F32)
            for i, ks in enumerate(key_sets):
                pv = pv + _dot(pr[hd_i * rph:(hd_i + 1) * rph, i * n:(i + 1) * n], ks(hd_i, 1).astype(BF16))
            pvs.append(pv)
        acc_ref[...] = alpha * acc_ref[...] + jnp.concatenate(pvs, axis=0)

    span = DIFF_PAGES * page_rows
    krel = j * span - past + lax.broadcasted_iota(jnp.int32, (1, span), 1)
    attend([lambda hd_i, kind, pg=pg: pg[pl.ds(kind * DIFF_HEADS + hd_i, page_rows, stride=DIFF_ROWS), :]
            for pg in pages], krel, True)

    @pl.when(j == n_steps - 1)
    def _():
        lane = lax.broadcasted_iota(jnp.int32, (1, LANES), 1)
        new_rows = lambda hd_i, kind: _pad_rows(
            new_ref[0, pl.ds(kind * DIFF_HEADS + hd_i, tn, stride=DIFF_ROWS), :], LANES)
        attend([new_rows], lane, lane < tn)
        o = acc_ref[...] / jnp.maximum(l_ref[...], 1e-30)
        lam = _diff_lambda(lqk_ref, lam_init)
        for hd_i in range(DIFF_HEADS):
            o1 = o[hd_i * rph:hd_i * rph + tn]
            o2 = o[hd_i * rph + tn:(hd_i + 1) * rph]
            od = _rms(o1 - lam * o2, gsub_ref[...]) * (1.0 - lam_init)
            o_ref[0, :, hd_i * DIFF_VD:(hd_i + 1) * DIFF_VD] = od


def _diff_sample(page_table, cache_r, qbd, new_diff, lqk, gsub, lam_init):
    bd, pps = page_table.shape
    tn = new_diff.shape[1] // DIFF_ROWS
    rows = 2 * DIFF_HEADS * tn
    n_steps = pps // DIFF_PAGES
    per_b = lambda shape: pl.BlockSpec((1,) + shape, lambda b, j, pt: (b,) + (0,) * len(shape))
    in_specs = _page_specs(DIFF_PAGES, cache_r.shape[1], 0, pps) + [
        per_b(qbd.shape[1:]), per_b(new_diff.shape[1:]),
        pl.BlockSpec(lqk.shape, lambda b, j, pt: (0, 0)),
        pl.BlockSpec(gsub.shape, lambda b, j, pt: (0, 0))]
    grid_spec = pltpu.PrefetchScalarGridSpec(
        num_scalar_prefetch=1, grid=(bd, n_steps), in_specs=in_specs,
        out_specs=per_b((tn, DV_W)),
        scratch_shapes=[pltpu.VMEM((rows, 1), F32), pltpu.VMEM((rows, 1), F32),
                        pltpu.VMEM((rows, DIFF_VD), F32)])
    return pl.pallas_call(
        functools.partial(_diff_sample_kernel, tn, lam_init, n_steps), grid_spec=grid_spec,
        out_shape=jax.ShapeDtypeStruct((bd, tn, DV_W), F32),
        compiler_params=_params(("arbitrary", "arbitrary")), name="diff_sample",
    )(page_table.reshape(-1), *([cache_r] * DIFF_PAGES), qbd, new_diff, lqk, gsub)


def _block_diag2(a):
    z = jnp.zeros_like(a)
    return jnp.concatenate([jnp.concatenate([a, z], axis=-1), jnp.concatenate([z, a], axis=-1)], axis=-2)


def _layer_weights(l, g_ffn1_pre, ffn1_w1, ffn1_w3, ffn1_w2, g_ffn1_post, g_mix_pre, w_in, cmp_w1,
                   cmp_b1, cmp_w2, lambda_qk, g_diff_subln, w_up_nsa, w_up_diff, w_out, g_mix_post,
                   g_ffn2_pre, ffn2_w1, ffn2_w3, ffn2_w2, g_ffn2_post):
    assert NSA_KV_HEADS == 2 and 2 * NSA_BLOCK == LANES
    row = lambda g: g[l][None, :]
    b16 = lambda w: w[l].astype(BF16)
    wi = w_in[l]
    o_ng = NQ_W + NKV_W
    o_dq = o_ng + NG_W
    o_mg = o_dq + 2 * DQ_W + DV_W
    wp = jnp.concatenate(
        [wi[:, :o_ng], wi[:, o_dq:o_mg], wi[:, o_ng:o_dq],
         jnp.zeros((wi.shape[0], LANES - NG_W), wi.dtype)], axis=1).astype(BF16)
    w1c = cmp_w1[l]
    return dict(
        ffn1=(row(g_ffn1_pre), b16(ffn1_w1), b16(ffn1_w3), b16(ffn1_w2), row(g_ffn1_post)),
        ffn2=(row(g_ffn2_pre), b16(ffn2_w1), b16(ffn2_w3), b16(ffn2_w2), row(g_ffn2_post)),
        gmix=row(g_mix_pre), wp=wp, wt=wi[:, NQ_W:o_ng].T.astype(BF16), wmg=wi[:, o_mg:].astype(BF16),
        w1r=w1c.reshape(2, NSA_BLOCK * HD, HD).astype(BF16), b1=cmp_b1[l][:, None, :],
        w2c=cmp_w2[l].astype(BF16),
        wd=_block_diag2(w1c.transpose(0, 2, 1, 3)).astype(BF16),
        bd2=_block_diag2(cmp_w2[l]).astype(BF16),
        b1t=jnp.tile(cmp_b1[l], (1, 2))[:, None, :],
        lqk=lambda_qk[l], gsub=row(g_diff_subln), wun=b16(w_up_nsa), wud=b16(w_up_diff),
        wout=b16(w_out), gmixpost=row(g_mix_post))


def _prompt_layer(x, W, lam_init):
    bsz, t, d = x.shape
    n = bsz * t
    (x1, q_sep, cmp_sep, nsa_new_t, win_new_t, selwin_t, diff_new, dq, dk, dv, ng) = _ffn_proj(
        x.reshape(n, d), *W["ffn1"], W["gmix"], W["wp"], W["wt"], tm=512, seq_len=t)
    xc = cmp_sep.reshape(2 * NSA_KV_HEADS, n // NSA_BLOCK, NSA_BLOCK * HD)
    ckcv = _compress_prompt(xc, W["w1r"], W["b1"], W["w2c"])
    o_nsa = _nsa_prompt(q_sep, ckcv, selwin_t, ng, bsz, t)
    o_diff = _diff_prompt(dq, dk, dv, W["lqk"], W["gsub"], lam_init, bsz, t)
    y = _merge_ffn(x1, o_nsa, o_diff, W["gmix"], W["wmg"], W["wun"], W["wud"], W["wout"],
                   W["gmixpost"], *W["ffn2"], tm=512)
    w_rows = min(NSA_WINDOW, t)
    new_nsa = nsa_new_t.reshape(bsz, 4, NSA_KV_HEADS, HD, t).transpose(0, 4, 1, 2, 3)
    new_win = win_new_t[:, :, t - w_rows:].reshape(bsz, 2, NSA_KV_HEADS, HD, w_rows).transpose(0, 4, 1, 2, 3)
    new_diff = diff_new.reshape(bsz, t, 2, DIFF_HEADS, 2 * HD)
    return y.reshape(bsz, t, d), (new_nsa, new_diff, new_win)


def _sample_layer(x, c_nsa, c_diff, win, page_table, W, lam_init):
    bd, tn, d = x.shape
    n = bd * tn
    n_pool, page_rows = c_nsa.shape[:2]
    pps = page_table.shape[1]
    past = pps * page_rows
    nb = -(-(past + tn) // NSA_BLOCK)
    assert page_rows == LANES and nb <= CK_ROWS
    (x1, q_sep, _, nsa_new, win_new, _, diff_new, dq, _, _, ng) = _ffn_proj(
        x.reshape(n, d), *W["ffn1"], W["gmix"], W["wp"], W["wt"], tm=n)
    nsa_t = c_nsa.transpose(0, 2, 3, 4, 1).reshape(n_pool, NSA_NEW_W, page_rows)
    diff_r = c_diff.reshape(n_pool, page_rows * DIFF_ROWS, LANES)
    win_t = win.transpose(0, 2, 3, 4, 1).reshape(bd, WIN_W, win.shape[1])
    nsa_new3 = nsa_new.reshape(bd, tn, NSA_NEW_W)

    new_cmp_t = jnp.pad(nsa_new3[:, :, :CMP_W].transpose(0, 2, 1), ((0, 0), (0, 0), (0, LANES - tn)))
    ck_pages = _compress_sample(page_table, nsa_t, new_cmp_t, W["wd"], W["b1t"], W["bd2"])
    slots = ck_pages.shape[2]
    ckcv = ck_pages.reshape(bd, 2 * NSA_KV_HEADS, slots * 2, HD)[:, :, :nb]
    ckcv = jnp.pad(ckcv, ((0, 0), (0, 0), (0, CK_ROWS - nb), (0, 0)))

    q_rows = q_sep.reshape(NSA_HEADS, bd, tn, HD).transpose(1, 0, 2, 3).reshape(bd, NSA_HEADS * tn, HD)
    kv_of_row = jnp.arange(NSA_HEADS * tn) // (NSA_GROUP * tn)
    qbd_n = jnp.concatenate(
        [jnp.where((kv_of_row == kvh)[None, :, None], q_rows, 0) for kvh in range(NSA_KV_HEADS)], axis=-1)
    gates = ng[:, :NG_W].reshape(bd, tn, NSA_HEADS, 3).transpose(0, 2, 1, 3).reshape(bd, NSA_HEADS * tn, 3)
    gates = jnp.pad(gates, ((0, 0), (0, 0), (0, LANES - 3)))
    o_nsa = _nsa_sample(page_table, nsa_t, qbd_n, q_rows, ckcv, nsa_new3[:, :, CMP_W:], win_t,
                        win_new.reshape(bd, tn, WIN_W), gates, nb)

    dq5 = dq.reshape(DIFF_HEADS, 2, bd, tn, HD).transpose(2, 0, 1, 3, 4)
    z = jnp.zeros_like(dq5[:, :, 0])
    qbd_d = jnp.concatenate([jnp.concatenate([dq5[:, :, 0], z], axis=-1),
                             jnp.concatenate([z, dq5[:, :, 1]], axis=-1)], axis=2)
    o_diff = _diff_sample(page_table, diff_r, qbd_d, diff_new.reshape(bd, tn * DIFF_ROWS, LANES),
                          W["lqk"], W["gsub"], lam_init)
    y = _merge_ffn(x1, o_nsa.reshape(n, NQ_W), o_diff.reshape(n, DV_W), W["gmix"], W["wmg"],
                   W["wun"], W["wud"], W["wout"], W["gmixpost"], *W["ffn2"], tm=n)
    new_nsa = nsa_new.reshape(bd, tn, 4, NSA_KV_HEADS, HD)
    new_diff = diff_new.reshape(bd, tn, 2, DIFF_HEADS, 2 * HD)
    new_win = jnp.concatenate([win, win_new.reshape(bd, tn, 2, NSA_KV_HEADS, HD)], axis=1)[:, tn:]
    return y.reshape(bd, tn, d), (new_nsa, new_diff, new_win)


def kernel(x_prompt, x_sample, cache_nsa_kv, cache_diff_kv, state_win_kv, page_table, g_ffn1_pre, ffn1_w1, ffn1_w3, ffn1_w2, g_ffn1_post, g_mix_pre, w_in, cmp_w1, cmp_b1, cmp_w2, lambda_qk, g_diff_subln, w_up_nsa, w_up_diff, w_out, g_mix_post, g_ffn2_pre, ffn2_w1, ffn2_w3, ffn2_w2, g_ffn2_post):
    depth = w_in.shape[0]
    y_p, y_s = x_prompt, x_sample
    outs = [[] for _ in range(6)]
    for l in range(depth):
        lam_init = 0.8 - 0.6 * math.exp(-0.3 * l)
        W = _layer_weights(l, g_ffn1_pre, ffn1_w1, ffn1_w3, ffn1_w2, g_ffn1_post, g_mix_pre, w_in,
                           cmp_w1, cmp_b1, cmp_w2, lambda_qk, g_diff_subln, w_up_nsa, w_up_diff,
                           w_out, g_mix_post, g_ffn2_pre, ffn2_w1, ffn2_w3, ffn2_w2, g_ffn2_post)
        y_p, (a_p, b_p, c_p) = _prompt_layer(y_p, W, lam_init)
        y_s, (a_s, b_s, c_s) = _sample_layer(y_s, cache_nsa_kv[l], cache_diff_kv[l], state_win_kv[l],
                                             page_table, W, lam_init)
        for lst, val in zip(outs, (a_p, a_s, b_p, b_s, c_p, c_s)):
            lst.append(val)
    return (y_p, y_s) + tuple(jnp.stack(o) for o in outs)
```

```python
import functools
import math

import jax
import jax.numpy as jnp
from jax import lax
from jax.experimental import pallas as pl
from jax.experimental.pallas import tpu as pltpu

F32 = jnp.float32
BF16 = jnp.bfloat16

HD = 64
NSA_HEADS = 8
NSA_KV_HEADS = 2
NSA_GROUP = NSA_HEADS // NSA_KV_HEADS
NSA_BLOCK = 64
NSA_TOPN = 16
NSA_WINDOW = 512
DIFF_HEADS = 4
DIFF_VD = 2 * HD
EPS = 1e-6
FORCE_SCORE = 1e4
NEG = -1e30
LOG2E = 1.0 / math.log(2.0)
Q_SCALE = HD ** -0.5 * LOG2E
LANES = 128
SUBLANES = 8
FF_CHUNK = 256
VMEM_LIMIT = 56 * 1024 * 1024

NQ_W = NSA_HEADS * HD
NKV_W = 6 * NSA_KV_HEADS * HD
NSA_NEW_W = 4 * NSA_KV_HEADS * HD
CMP_W = 2 * NSA_KV_HEADS * HD
WIN_W = 2 * NSA_KV_HEADS * HD
NG_W = 3 * NSA_HEADS
DQ_W = DIFF_HEADS * 2 * HD
DV_W = DIFF_HEADS * DIFF_VD
DIFF_ROWS = 2 * DIFF_HEADS


def _nsa_slope2(head):
    return 2.0 ** (-8.0 * (head + 1) / NSA_HEADS) * LOG2E


def _dot(a, b):
    return jnp.dot(a, b, preferred_element_type=F32)


def _dot_nt(a, b):
    return lax.dot_general(a, b, (((1,), (1,)), ((), ())), preferred_element_type=F32)


def _rms(x, g):
    return x * lax.rsqrt(jnp.mean(x * x, axis=-1, keepdims=True) + EPS) * g


def _gelu_tanh(x):
    c = math.sqrt(2.0 / math.pi)
    return 0.5 * x * (1.0 + jnp.tanh(c * (x + 0.044715 * (x * x * x))))


def _const_spec(shape):
    nd = len(shape)
    return pl.BlockSpec(shape, lambda *_: (0,) * nd, pipeline_mode=pl.Buffered(1))


def _params(sem):
    return pltpu.CompilerParams(dimension_semantics=sem, vmem_limit_bytes=VMEM_LIMIT)


def _softmax2(s, valid, axis=-1):
    s = jnp.where(valid, s, NEG)
    p = jnp.where(valid, jnp.exp2(s - jnp.max(s, axis=axis, keepdims=True)), 0.0)
    return p / jnp.maximum(jnp.sum(p, axis=axis, keepdims=True), 1e-30)


def _ffn_half_step(x, g_pre, w1_ref, w3_ref, w2_ref, g_post):
    xn = _rms(x, g_pre).astype(BF16)
    d_ff = w1_ref.shape[1]
    acc = jnp.zeros(x.shape, F32)
    for c in range(d_ff // FF_CHUNK):
        cs = slice(c * FF_CHUNK, (c + 1) * FF_CHUNK)
        h1 = _dot(xn, w1_ref[:, cs])
        h3 = _dot(xn, w3_ref[:, cs])
        gate = (h1 * jax.nn.sigmoid(h1) * h3).astype(BF16)
        acc = acc + _dot(gate, w2_ref[cs, :])
    return x + 0.5 * _rms(acc, g_post)


def _store_heads(ref, x, width, scale=None):
    for i in range(ref.shape[0]):
        piece = x[:, i * width:(i + 1) * width]
        ref[i] = (piece if scale is None else piece * scale).astype(ref.dtype)


def _store_diff_rows(ref, dk, dv, tm):
    for i in range(DIFF_HEADS):
        ref[pl.ds(i, tm, stride=DIFF_ROWS), :] = dk[:, i * 2 * HD:(i + 1) * 2 * HD]
        ref[pl.ds(DIFF_HEADS + i, tm, stride=DIFF_ROWS), :] = dv[:, i * DIFF_VD:(i + 1) * DIFF_VD]


def _ffn_proj_kernel(prompt, x_ref, gpre_ref, w1_ref, w3_ref, w2_ref, gpost_ref, gmix_ref,
                     wp_ref, wt_ref, x1_ref, q_ref, dq_ref, diff_new_ref, *out_refs):
    tm = x_ref.shape[0]
    x1 = _ffn_half_step(x_ref[...], gpre_ref[...], w1_ref, w3_ref, w2_ref, gpost_ref[...])
    x1_ref[...] = x1
    h = _rms(x1, gmix_ref[...]).astype(BF16)
    o_nkv = NQ_W
    o_dq = o_nkv + NKV_W
    o_dk = o_dq + DQ_W
    o_dv = o_dk + DQ_W
    o_ng = o_dv + DV_W
    _store_heads(q_ref, _dot(h, wp_ref[:, :NQ_W]), HD, Q_SCALE)
    _store_heads(dq_ref, _dot(h, wp_ref[:, o_dq:o_dk]), HD, Q_SCALE)
    nkv = _dot(h, wp_ref[:, o_nkv:o_dq])
    dk = _dot(h, wp_ref[:, o_dk:o_dv])
    dv = _dot(h, wp_ref[:, o_dv:o_ng])
    _store_diff_rows(diff_new_ref, dk, dv, tm)
    if prompt:
        (cmp_ref, nsa_new_ref, win_new_ref, krow_ref, vt_ref, dk_ref, dvt_ref, ngt_ref) = out_refs
        _store_heads(cmp_ref, nkv[:, :CMP_W], HD)
        _store_heads(dk_ref, dk, HD)
        sel_k = nkv[:, CMP_W:CMP_W + NSA_KV_HEADS * HD]
        win_k = nkv[:, NSA_NEW_W:NSA_NEW_W + NSA_KV_HEADS * HD]
        _store_heads(krow_ref, jnp.concatenate([sel_k, win_k], axis=1), HD)
        zt = _dot_nt(wt_ref[...], h)
        nsa_new_ref[0] = zt[:NSA_NEW_W]
        win_new_ref[0] = zt[NSA_NEW_W:NKV_W]
        for kvh in range(NSA_KV_HEADS):
            r_sel = CMP_W + (NSA_KV_HEADS + kvh) * HD
            r_win = NSA_NEW_W + (NSA_KV_HEADS + kvh) * HD
            vt_ref[kvh] = zt[r_sel:r_sel + HD].astype(BF16)
            vt_ref[NSA_KV_HEADS + kvh] = zt[r_win:r_win + HD].astype(BF16)
        for i in range(DIFF_HEADS):
            dvt_ref[i] = zt[NKV_W + i * DIFF_VD:NKV_W + (i + 1) * DIFF_VD].astype(BF16)
        ngt_ref[0] = jax.nn.sigmoid(zt[NKV_W + DV_W:])
    else:
        (nsa_new_ref, win_new_ref, ng_ref) = out_refs
        nsa_new_ref[...] = nkv[:, :NSA_NEW_W]
        win_new_ref[...] = nkv[:, NSA_NEW_W:]
        ng_ref[...] = jax.nn.sigmoid(_dot(h, wp_ref[:, o_ng:]))


def _ffn_proj(x, gpre, w1, w3, w2, gpost, gmix, wp, wt, tm, seq_len=None):
    n, d = x.shape
    d_ff = w1.shape[1]
    row = lambda w: pl.BlockSpec((tm, w), lambda i: (i, 0))
    sep = lambda k, w: pl.BlockSpec((k, tm, w), lambda i: (0, i, 0))
    sds = jax.ShapeDtypeStruct
    out_shape = [sds((n, d), F32),
                 sds((NSA_HEADS, n, HD), BF16),
                 sds((2 * DIFF_HEADS, n, HD), BF16),
                 sds((n * DIFF_ROWS, LANES), F32)]
    out_specs = [row(d), sep(NSA_HEADS, HD), sep(2 * DIFF_HEADS, HD),
                 pl.BlockSpec((tm * DIFF_ROWS, LANES), lambda i: (i, 0))]
    if seq_len is not None:
        tiles = seq_len // tm
        nseq = n // seq_len
        fmaj = lambda w: pl.BlockSpec((1, w, tm), lambda i: (i // tiles, 0, i % tiles))
        tsep = lambda k, w: pl.BlockSpec((k, w, tm), lambda i: (0, 0, i))
        out_shape += [sds((2 * NSA_KV_HEADS, n, HD), BF16),
                      sds((nseq, NSA_NEW_W, seq_len), F32),
                      sds((nseq, WIN_W, seq_len), F32),
                      sds((2 * NSA_KV_HEADS, n, HD), BF16),
                      sds((2 * NSA_KV_HEADS, HD, n), BF16),
                      sds((2 * DIFF_HEADS, n, HD), BF16),
                      sds((DIFF_HEADS, DIFF_VD, n), BF16),
                      sds((nseq, LANES, seq_len), F32)]
        out_specs += [sep(2 * NSA_KV_HEADS, HD), fmaj(NSA_NEW_W), fmaj(WIN_W), sep(2 * NSA_KV_HEADS, HD),
                      tsep(2 * NSA_KV_HEADS, HD), sep(2 * DIFF_HEADS, HD), tsep(DIFF_HEADS, DIFF_VD),
                      fmaj(LANES)]
    else:
        out_shape += [sds((n, NSA_NEW_W), F32), sds((n, WIN_W), F32), sds((n, LANES), F32)]
        out_specs += [row(NSA_NEW_W), row(WIN_W), row(LANES)]
    in_specs = [row(d), _const_spec((1, d)), _const_spec((d, d_ff)), _const_spec((d, d_ff)),
                _const_spec((d_ff, d)), _const_spec((1, d)), _const_spec((1, d)),
                _const_spec(wp.shape), _const_spec(wt.shape)]
    return pl.pallas_call(
        functools.partial(_ffn_proj_kernel, seq_len is not None), grid=(n // tm,), in_specs=in_specs,
        out_specs=out_specs, out_shape=out_shape, compiler_params=_params(("parallel",)),
        name="ffn_proj",
    )(x, gpre, w1, w3, w2, gpost, gmix, wp, wt)


def _compress_kernel(x_ref, w1_ref, b1_ref, w2_ref, o_ref):
    hid = _gelu_tanh(_dot(x_ref[0], w1_ref[0]) + b1_ref[0])
    o_ref[0] = _dot(hid.astype(BF16), w2_ref[0])


def _compress_prompt(xc, w1r, b1, w2):
    k, nb, kd = xc.shape
    return pl.pallas_call(
        _compress_kernel, grid=(k,),
        in_specs=[pl.BlockSpec((1, nb, kd), lambda i: (i, 0, 0)),
                  pl.BlockSpec((1, kd, HD), lambda i: (i // NSA_KV_HEADS, 0, 0)),
                  pl.BlockSpec((1, 1, HD), lambda i: (i // NSA_KV_HEADS, 0, 0)),
                  pl.BlockSpec((1, HD, HD), lambda i: (i // NSA_KV_HEADS, 0, 0))],
        out_specs=pl.BlockSpec((1, nb, HD), lambda i: (i, 0, 0)),
        out_shape=jax.ShapeDtypeStruct((k, nb, HD), F32),
        compiler_params=_params(("parallel",)), name="nsa_compress_prompt",
    )(xc, w1r, b1, w2)


NSA_TQ = 128
NSA_TK = 512


def _topn_mask_t(imp_t):
    nb, nt = imp_t.shape
    rank = jnp.zeros((nb, nt), F32)
    jidx = lax.broadcasted_iota(jnp.int32, (nb, nt), 0)
    for i in range(nb):
        vi = jnp.broadcast_to(imp_t[i:i + 1, :], (nb, nt))
        ahead = (vi > imp_t) | ((vi == imp_t) & (jidx > i))
        rank = rank + jnp.where(ahead, 1.0, 0.0)
    return jnp.where((rank < NSA_TOPN) & (imp_t >= 0.0), 1.0, 0.0)


def _nsa_prompt_kernel(q_ref, ck_ref, cvt_ref, krow_ref, vt_ref, ngt_ref, o_ref):
    qi = pl.program_id(1)
    s0 = qi * NSA_TQ
    nb = ck_ref.shape[2]
    cols = NSA_GROUP * NSA_TQ
    lane = lax.broadcasted_iota(jnp.int32, (1, cols), 1)
    tok = lane % NSA_TQ
    gidx = lane // NSA_TQ
    tok1 = lax.broadcasted_iota(jnp.int32, (1, NSA_TQ), 1)
    key_i = lax.broadcasted_iota(jnp.int32, (NSA_TK, cols), 0)
    span = NSA_WINDOW + NSA_TQ
    w0 = pl.multiple_of(jnp.maximum(s0 - NSA_WINDOW, 0), NSA_TQ)
    wrel = w0 - s0 + lax.broadcasted_iota(jnp.int32, (span, cols), 0)
    wdist = tok - wrel
    wvalid = (wdist >= 0) & (wdist < NSA_WINDOW)
    blk_c = lax.broadcasted_iota(jnp.int32, (nb, cols), 0)
    end_c = (blk_c + 1) * NSA_BLOCK - 1 - s0
    blk_t = lax.broadcasted_iota(jnp.int32, (nb, NSA_TQ), 0)
    cur = (s0 + tok1) // NSA_BLOCK
    for kvh in range(NSA_KV_HEADS):
        slope = jnp.zeros((1, cols), F32)
        for g in range(NSA_GROUP):
            slope = jnp.where(gidx == g, _nsa_slope2(kvh * NSA_GROUP + g), slope)
        q = q_ref[kvh * NSA_GROUP:(kvh + 1) * NSA_GROUP].reshape(cols, HD)

        ck = ck_ref[0, kvh].astype(BF16)
        cv_t = cvt_ref[0, NSA_KV_HEADS + kvh].astype(BF16)
        p_t = _softmax2(_dot_nt(ck, q) + slope * end_c.astype(F32), tok >= end_c, axis=0)
        o_c = _dot(cv_t, p_t.astype(BF16))

        imp = p_t[:, 0:NSA_TQ]
        for g in range(1, NSA_GROUP):
            imp = imp + p_t[:, g * NSA_TQ:(g + 1) * NSA_TQ]
        forced = (blk_t == cur) | (blk_t == 0)
        imp = jnp.where(forced, FORCE_SCORE, jnp.where(blk_t <= cur, imp, -1.0))
        sel_t = _topn_mask_t(imp).astype(BF16)

        bias0 = slope * key_i.astype(F32)

        def sel_tile(kt, carry, diagonal):
            m, l, acc = carry
            k0 = pl.multiple_of(kt * NSA_TK, NSA_TK)
            k = krow_ref[kvh, pl.ds(k0, NSA_TK), :]
            v_t = vt_ref[kvh, :, pl.ds(k0, NSA_TK)]
            expand = (lax.broadcasted_iota(jnp.int32, (NSA_TK, nb), 1)
                      == (k0 + lax.broadcasted_iota(jnp.int32, (NSA_TK, nb), 0)) // NSA_BLOCK)
            hit = _dot(expand.astype(BF16), sel_t)
            valid = jnp.concatenate([hit] * NSA_GROUP, axis=1) > 0.5
            if diagonal:
                valid = valid & (k0 - s0 + key_i <= tok)
            t = jnp.where(valid, _dot_nt(k, q) + bias0, NEG)
            shift = slope * (k0 - s0).astype(F32)
            m_new = jnp.maximum(m, jnp.max(t, axis=0, keepdims=True) + shift)
            pr = jnp.where(valid, jnp.exp2(t - (m_new - shift)), 0.0)
            alpha = jnp.exp2(m - m_new)
            l = alpha * l + jnp.sum(pr, axis=0, keepdims=True)
            acc = alpha * acc + _dot(v_t, pr.astype(BF16))
            return m_new, l, acc

        n_full = s0 // NSA_TK
        init = (jnp.full((1, cols), NEG, F32), jnp.zeros((1, cols), F32), jnp.zeros((HD, cols), F32))
        carry = lax.fori_loop(0, n_full, functools.partial(sel_tile, diagonal=False), init)
        _, l_s, acc_s = sel_tile(n_full, carry, True)
        o_s = acc_s / jnp.maximum(l_s, 1e-30)

        kw = krow_ref[NSA_KV_HEADS + kvh, pl.ds(w0, span), :]
        vw_t = vt_ref[NSA_KV_HEADS + kvh, :, pl.ds(w0, span)]
        pw = _softmax2(_dot_nt(kw, q) + slope * wrel.astype(F32), wvalid, axis=0)
        o_w = _dot(vw_t, pw.astype(BF16))

        gated = []
        for g in range(NSA_GROUP):
            row = 3 * (kvh * NSA_GROUP + g)
            c = slice(g * NSA_TQ, (g + 1) * NSA_TQ)
            gated.append(ngt_ref[0, row:row + 1, :] * o_c[:, c] + ngt_ref[0, row + 1:row + 2, :] * o_s[:, c]
                         + ngt_ref[0, row + 2:row + 3, :] * o_w[:, c])
        for g in range(0, NSA_GROUP, 2):
            c0 = (kvh * NSA_GROUP + g) * HD
            pair = jnp.concatenate([gated[g], gated[g + 1]], axis=0)
            o_ref[:, c0:c0 + 2 * HD] = pair.T.astype(o_ref.dtype)


def _nsa_prompt(q_sep, ck, cv_t, k_rows, v_t, ng_t, bsz, t):
    nq = t // NSA_TQ
    nb = t // NSA_BLOCK
    n = bsz * t
    return pl.pallas_call(
        _nsa_prompt_kernel, grid=(bsz, nq),
        in_specs=[pl.BlockSpec((NSA_HEADS, NSA_TQ, HD), lambda b, i: (0, b * nq + i, 0)),
                  pl.BlockSpec((1, 2 * NSA_KV_HEADS, nb, HD), lambda b, i: (b, 0, 0, 0)),
                  pl.BlockSpec((1, 2 * NSA_KV_HEADS, HD, nb), lambda b, i: (b, 0, 0, 0)),
                  pl.BlockSpec((2 * NSA_KV_HEADS, t, HD), lambda b, i: (0, b, 0)),
                  pl.BlockSpec((2 * NSA_KV_HEADS, HD, t), lambda b, i: (0, 0, b)),
                  pl.BlockSpec((1, LANES, NSA_TQ), lambda b, i: (b, 0, i))],
        out_specs=pl.BlockSpec((NSA_TQ, NQ_W), lambda b, i: (b * nq + i, 0)),
        out_shape=jax.ShapeDtypeStruct((n, NQ_W), BF16),
        compiler_params=_params(("parallel", "parallel")), name="nsa_prompt",
    )(q_sep, ck, cv_t, k_rows, v_t, ng_t)


DIFF_T = 512


def _diff_lambda(lqk_ref, lam_init):
    l = lqk_ref[...]
    a = jnp.sum(l[0:1] * l[1:2], axis=-1, keepdims=True)
    b = jnp.sum(l[2:3] * l[3:4], axis=-1, keepdims=True)
    return jnp.exp(a) - jnp.exp(b) + lam_init


def _pow2_neg(n_vec):
    return lax.bitcast_convert_type((127 - n_vec) << 23, F32)


def _diff_prompt_kernel(lam_init, q_ref, k_ref, vt_ref, lqk_ref, gsub_ref, o_ref):
    h = pl.program_id(1)
    qi = pl.program_id(2)
    s0 = qi * DIFF_T
    slope = _pow2_neg(jnp.full((1, 1), 2 * (h + 1), jnp.int32)) * LOG2E
    key_i = lax.broadcasted_iota(jnp.int32, (DIFF_T, DIFF_T), 0)
    causal = key_i <= lax.broadcasted_iota(jnp.int32, (DIFF_T, DIFF_T), 1)
    bias0 = slope * key_i.astype(F32)
    q1 = q_ref[0]
    q2 = q_ref[1]

    def tile(kt, carry, diagonal):
        k0 = pl.multiple_of(kt * DIFF_T, DIFF_T)
        shift = slope * (k0 - s0).astype(F32)
        v_t = vt_ref[0, :, pl.ds(k0, DIFF_T)]

        def one(q, k, state):
            m, l, acc = state
            t = _dot_nt(k, q) + bias0
            if diagonal:
                t = jnp.where(causal, t, NEG)
            m_new = jnp.maximum(m, jnp.max(t, axis=0, keepdims=True) + shift)
            pr = jnp.exp2(t - (m_new - shift))
            alpha = jnp.exp2(m - m_new)
            l = alpha * l + jnp.sum(pr, axis=0, keepdims=True)
            return m_new, l, alpha * acc + _dot(v_t, pr.astype(BF16))

        return (one(q1, k_ref[0, pl.ds(k0, DIFF_T), :], carry[0]),
                one(q2, k_ref[1, pl.ds(k0, DIFF_T), :], carry[1]))

    state0 = (jnp.full((1, DIFF_T), NEG, F32), jnp.zeros((1, DIFF_T), F32),
              jnp.zeros((DIFF_VD, DIFF_T), F32))
    carry = lax.fori_loop(0, qi, functools.partial(tile, diagonal=False), (state0, state0))
    (_, l1, acc1), (_, l2, acc2) = tile(qi, carry, True)
    lam = _diff_lambda(lqk_ref, lam_init)
    o = (acc1 / l1 - lam * (acc2 / l2)).T
    o_ref[...] = (_rms(o, gsub_ref[...]) * (1.0 - lam_init)).astype(o_ref.dtype)


def _diff_prompt(dq, dk, dv_t, lqk, gsub, lam_init, bsz, t):
    nq = t // DIFF_T
    n = bsz * t
    return pl.pallas_call(
        functools.partial(_diff_prompt_kernel, lam_init), grid=(bsz, DIFF_HEADS, nq),
        in_specs=[pl.BlockSpec((2, DIFF_T, HD), lambda b, h, i: (h, b * nq + i, 0)),
                  pl.BlockSpec((2, t, HD), lambda b, h, i: (h, b, 0)),
                  pl.BlockSpec((1, DIFF_VD, t), lambda b, h, i: (h, 0, b)),
                  pl.BlockSpec(lqk.shape, lambda b, h, i: (0, 0)),
                  pl.BlockSpec(gsub.shape, lambda b, h, i: (0, 0))],
        out_specs=pl.BlockSpec((DIFF_T, DIFF_VD), lambda b, h, i: (b * nq + i, h)),
        out_shape=jax.ShapeDtypeStruct((n, DV_W), BF16),
        compiler_params=_params(("parallel", "parallel", "parallel")), name="diff_prompt",
    )(dq, dk, dv_t, lqk, gsub)


def _merge_ffn_kernel(x_ref, on_ref, od_ref, gmix_ref, wmg_ref, wun_ref, wud_ref, wout_ref,
                      gmixpost_ref, gpre_ref, w1_ref, w3_ref, w2_ref, gpost_ref, y_ref):
    x1 = x_ref[...]
    d = x1.shape[1]
    h = _rms(x1, gmix_ref[...]).astype(BF16)
    up_n = _dot(on_ref[...].astype(BF16), wun_ref[...])
    up_d = _dot(od_ref[...].astype(BF16), wud_ref[...])
    m = (jax.nn.sigmoid(_dot(h, wmg_ref[:, :d])) * up_n
         + jax.nn.sigmoid(_dot(h, wmg_ref[:, d:])) * up_d)
    out = _dot(m.astype(BF16), wout_ref[...])
    x2 = x1 + _rms(out, gmixpost_ref[...])
    y_ref[...] = _ffn_half_step(x2, gpre_ref[...], w1_ref, w3_ref, w2_ref, gpost_ref[...])


def _merge_ffn(x1, o_nsa, o_diff, gmix, wmg, wun, wud, wout, gmixpost, gpre, w1, w3, w2, gpost, tm):
    n, d = x1.shape
    d_ff = w1.shape[1]
    row = lambda w: pl.BlockSpec((tm, w), lambda i: (i, 0))
    in_specs = [row(d), row(NQ_W), row(DV_W), _const_spec((1, d)), _const_spec(wmg.shape),
                _const_spec(wun.shape), _const_spec(wud.shape), _const_spec(wout.shape),
                _const_spec((1, d)), _const_spec((1, d)), _const_spec((d, d_ff)),
                _const_spec((d, d_ff)), _const_spec((d_ff, d)), _const_spec((1, d))]
    return pl.pallas_call(
        _merge_ffn_kernel, grid=(n // tm,), in_specs=in_specs, out_specs=row(d),
        out_shape=jax.ShapeDtypeStruct((n, d), F32),
        compiler_params=_params(("parallel",)), name="merge_ffn",
    )(x1, o_nsa, o_diff, gmix, wmg, wun, wud, wout, gmixpost, gpre, w1, w3, w2, gpost)


CMP_PAGES = 16
SEL_PAGES = 16
DIFF_PAGES = 16
CK_ROWS = 384


def _page_specs(n_pages, block_rows, half, pages_per_seq):
    def make(i):
        def imap(b, j, pt):
            return (pt[b * pages_per_seq + j * n_pages + i], half, 0)
        return pl.BlockSpec((None, block_rows, LANES), imap)
    return [make(i) for i in range(n_pages)]


def _compress_sample_kernel(n_steps, slots, pt_ref, *refs):
    pages = refs[:CMP_PAGES]
    new_ref, wd_ref, b1_ref, w2_ref, o_ref, xs_ref = refs[CMP_PAGES:]
    j = pl.program_id(1)
    n_pages = n_steps * CMP_PAGES

    @pl.when(j == 0)
    def _():
        xs_ref[pl.ds(n_pages * CMP_W, CMP_W), :] = new_ref[0]
        xs_ref[pl.ds((n_pages + 1) * CMP_W, (slots - n_pages - 1) * CMP_W), :] = jnp.zeros(
            ((slots - n_pages - 1) * CMP_W, LANES), F32)

    for i, pg in enumerate(pages):
        r0 = pl.multiple_of((j * CMP_PAGES + i) * CMP_W, CMP_W)
        xs_ref[pl.ds(r0, CMP_W), :] = pg[...]

    @pl.when(j == n_steps - 1)
    def _():
        for c in range(2 * NSA_KV_HEADS):
            kind = c // NSA_KV_HEADS
            lhs = jnp.concatenate(
                [xs_ref[pl.ds(c * HD + d, slots, stride=CMP_W), :].astype(BF16) for d in range(HD)],
                axis=1)
            hid = _gelu_tanh(_dot(lhs, wd_ref[kind]) + b1_ref[kind]).astype(BF16)
            o_ref[0, c] = _dot(hid, w2_ref[kind])


def _compress_sample(page_table, cache_t, new_cmp_t, wd, b1t, bd2):
    bd, pps = page_table.shape
    slots = -(-(pps + 1) // SUBLANES) * SUBLANES
    n_steps = pps // CMP_PAGES
    in_specs = _page_specs(CMP_PAGES, CMP_W, 0, pps) + [
        pl.BlockSpec((1, CMP_W, LANES), lambda b, j, pt: (b, 0, 0)),
        pl.BlockSpec(wd.shape, lambda b, j, pt: (0, 0, 0), pipeline_mode=pl.Buffered(1)),
        pl.BlockSpec(b1t.shape, lambda b, j, pt: (0, 0, 0), pipeline_mode=pl.Buffered(1)),
        pl.BlockSpec(bd2.shape, lambda b, j, pt: (0, 0, 0), pipeline_mode=pl.Buffered(1))]
    grid_spec = pltpu.PrefetchScalarGridSpec(
        num_scalar_prefetch=1, grid=(bd, n_steps), in_specs=in_specs,
        out_specs=pl.BlockSpec((1, 2 * NSA_KV_HEADS, slots, LANES), lambda b, j, pt: (b, 0, 0, 0)),
        scratch_shapes=[pltpu.VMEM((slots * CMP_W, LANES), F32)])
    return pl.pallas_call(
        functools.partial(_compress_sample_kernel, n_steps, slots), grid_spec=grid_spec,
        out_shape=jax.ShapeDtypeStruct((bd, 2 * NSA_KV_HEADS, slots, LANES), F32),
        compiler_params=_params(("arbitrary", "arbitrary")), name="nsa_compress_sample",
    )(page_table.reshape(-1), *([cache_t] * CMP_PAGES), new_cmp_t, wd, b1t, bd2)


def _head_diag(full, n_heads, rows_per_head, width):
    return jnp.concatenate(
        [full[h * rows_per_head:(h + 1) * rows_per_head, h * width:(h + 1) * width]
         for h in range(n_heads)], axis=0)


def _pad_rows(x, n):
    return jnp.concatenate([x, jnp.zeros((n - x.shape[0], x.shape[1]), x.dtype)], axis=0)


def _nsa_sample_kernel(tn, nb, n_steps, pt_ref, *refs):
    pages = refs[:SEL_PAGES]
    (qbd_ref, q_ref, ckcv_ref, new_ref, win_ref, wnew_ref, gate_ref, o_ref,
     sel_ref, oc_ref, m_ref, l_ref, acc_ref) = refs[SEL_PAGES:]
    j = pl.program_id(1)
    page_rows = pages[0].shape[1]
    past = n_steps * SEL_PAGES * page_rows
    rows = NSA_HEADS * tn
    rpk = NSA_GROUP * tn
    kvw = NSA_KV_HEADS * HD
    ridx = lax.broadcasted_iota(jnp.int32, (rows, 1), 0)
    tok = ridx % tn
    head = ridx // tn
    slope = jnp.zeros((rows, 1), F32)
    for hd_i in range(NSA_HEADS):
        slope = jnp.where(head == hd_i, _nsa_slope2(hd_i), slope)
    qbd = qbd_ref[0]

    @pl.when(j == 0)
    def _():
        blk = lax.broadcasted_iota(jnp.int32, (1, CK_ROWS), 1)
        end_c = (blk + 1) * NSA_BLOCK - 1 - past
        imps, ocs = [], []
        for kvh in range(NSA_KV_HEADS):
            r = slice(kvh * rpk, (kvh + 1) * rpk)
            ck = ckcv_ref[0, kvh].astype(BF16)
            cv = ckcv_ref[0, NSA_KV_HEADS + kvh].astype(BF16)
            p = _softmax2(_dot_nt(q_ref[0, r], ck) + slope[r] * end_c.astype(F32),
                          (tok[r] >= end_c) & (blk < nb))
            ocs.append(_dot(p.astype(BF16), cv))
            imp = p[0:tn]
            for g in range(1, NSA_GROUP):
                imp = imp + p[g * tn:(g + 1) * tn]
            imps.append(imp)
        oc_ref[...] = jnp.concatenate(ocs, axis=0)
        imp = jnp.concatenate(imps, axis=0)
        tq = past + lax.broadcasted_iota(jnp.int32, (NSA_KV_HEADS * tn, 1), 0) % tn
        cur = tq // NSA_BLOCK
        forced = (blk == cur) | (blk == 0)
        imp = jnp.where(forced, FORCE_SCORE, jnp.where(blk <= cur, imp, -1.0))
        imp = jnp.where(blk < nb, imp, -3.0)

        def pick(_, carry):
            vals, sel = carry
            top = jnp.max(vals, axis=-1, keepdims=True)
            first = jnp.min(jnp.where(vals == top, blk, CK_ROWS), axis=-1, keepdims=True)
            hit = blk == first
            sel = jnp.where(hit & (top >= 0.0), 1.0, sel)
            return jnp.where(hit, -2.0, vals), sel

        _, sel = lax.fori_loop(0, min(NSA_TOPN, nb), pick, (imp, jnp.zeros(imp.shape, F32)))
        sel_rows = jnp.concatenate(
            [sel[kvh * tn:(kvh + 1) * tn] for kvh in range(NSA_KV_HEADS) for _ in range(NSA_GROUP)],
            axis=0)
        for tile in range(CK_ROWS // LANES):
            sel_ref[tile] = sel_rows[:, tile * LANES:(tile + 1) * LANES]
        m_ref[...] = jnp.full(m_ref.shape, NEG, F32)
        l_ref[...] = jnp.zeros(l_ref.shape, F32)
        acc_ref[...] = jnp.zeros(acc_ref.shape, F32)

    def attend(scores, values, krel, blk_rel, sel_tile, extra_valid):
        n = scores.shape[1]
        expand = (lax.broadcasted_iota(jnp.int32, (LANES, n), 0) == blk_rel).astype(BF16)
        hit = _dot(sel_tile.astype(BF16), expand)
        valid = (hit > 0.5) & (krel <= tok) & extra_valid
        sc = jnp.where(valid, scores + slope * krel.astype(F32), NEG)
        m_old = m_ref[...]
        m_new = jnp.maximum(m_old, jnp.max(sc, axis=-1, keepdims=True))
        pr = jnp.where(valid, jnp.exp2(sc - m_new), 0.0)
        alpha = jnp.exp2(m_old - m_new)
        m_ref[...] = m_new
        l_ref[...] = alpha * l_ref[...] + jnp.sum(pr, axis=-1, keepdims=True)
        acc_ref[...] = alpha * acc_ref[...] + values(pr.astype(BF16))

    span = SEL_PAGES * page_rows
    blk0 = j * (span // NSA_BLOCK)
    tile = blk0 // LANES
    lane = lax.broadcasted_iota(jnp.int32, (1, span), 1)
    scores = jnp.concatenate([_dot(qbd, pg[:kvw, :].astype(BF16)) for pg in pages], axis=1)

    def page_values(pr):
        out = jnp.zeros((rows, kvw), F32)
        for i, pg in enumerate(pages):
            out = out + _dot_nt(pr[:, i * page_rows:(i + 1) * page_rows], pg[kvw:, :].astype(BF16))
        return out

    attend(scores, page_values, j * span - past + lane, blk0 - tile * LANES + lane // NSA_BLOCK,
           sel_ref[tile], True)

    @pl.when(j == n_steps - 1)
    def _():
        lane1 = lax.broadcasted_iota(jnp.int32, (1, LANES), 1)
        newk = _pad_rows(new_ref[0, :, :kvw], LANES).astype(BF16)
        newv = _pad_rows(new_ref[0, :, kvw:], LANES).astype(BF16)
        nblk0 = past // NSA_BLOCK
        ntile = nblk0 // LANES
        attend(_dot_nt(qbd, newk), lambda pr: _dot(pr, newv), lane1,
               nblk0 - ntile * LANES + lane1 // NSA_BLOCK, sel_ref[ntile], lane1 < tn)
        o_s = _head_diag(acc_ref[...] / jnp.maximum(l_ref[...], 1e-30), NSA_KV_HEADS, rpk, HD)

        wb = win_ref.shape[2]
        wnk = _pad_rows(wnew_ref[0, :, :kvw], LANES).astype(BF16)
        wnv = _pad_rows(wnew_ref[0, :, kvw:], LANES).astype(BF16)
        sc = jnp.concatenate([_dot(qbd, win_ref[0, :kvw, :].astype(BF16)), _dot_nt(qbd, wnk)], axis=1)
        lane_w = lax.broadcasted_iota(jnp.int32, (1, wb + LANES), 1)
        krel = lane_w - wb
        dist = tok - krel
        valid = (dist >= 0) & (dist < NSA_WINDOW) & (krel + past >= 0) & (lane_w < wb + tn)
        pr = _softmax2(sc + slope * krel.astype(F32), valid).astype(BF16)
        o_w = _head_diag(_dot_nt(pr[:, :wb], win_ref[0, kvw:, :].astype(BF16)) + _dot(pr[:, wb:], wnv),
                         NSA_KV_HEADS, rpk, HD)

        gt = gate_ref[0]
        out = gt[:, 0:1] * oc_ref[...] + gt[:, 1:2] * o_s + gt[:, 2:3] * o_w
        for hd_i in range(NSA_HEADS):
            o_ref[0, :, hd_i * HD:(hd_i + 1) * HD] = out[hd_i * tn:(hd_i + 1) * tn]


def _nsa_sample(page_table, cache_t, qbd, q_rows, ckcv, new_sel, win_t, win_new, gates, nb):
    bd, pps = page_table.shape
    tn = new_sel.shape[1]
    rows = NSA_HEADS * tn
    kvw = NSA_KV_HEADS * HD
    n_steps = pps // SEL_PAGES
    per_b = lambda shape: pl.BlockSpec((1,) + shape, lambda b, j, pt: (b,) + (0,) * len(shape))
    in_specs = _page_specs(SEL_PAGES, 2 * kvw, 1, pps) + [
        per_b((rows, kvw)), per_b((rows, HD)), per_b((2 * NSA_KV_HEADS, CK_ROWS, HD)),
        per_b((tn, 2 * kvw)), per_b(win_t.shape[1:]), per_b((tn, 2 * kvw)), per_b((rows, LANES))]
    grid_spec = pltpu.PrefetchScalarGridSpec(
        num_scalar_prefetch=1, grid=(bd, n_steps), in_specs=in_specs,
        out_specs=per_b((tn, NQ_W)),
        scratch_shapes=[pltpu.VMEM((CK_ROWS // LANES, rows, LANES), F32), pltpu.VMEM((rows, HD), F32),
                        pltpu.VMEM((rows, 1), F32), pltpu.VMEM((rows, 1), F32),
                        pltpu.VMEM((rows, kvw), F32)])
    return pl.pallas_call(
        functools.partial(_nsa_sample_kernel, tn, nb, n_steps), grid_spec=grid_spec,
        out_shape=jax.ShapeDtypeStruct((bd, tn, NQ_W), F32),
        compiler_params=_params(("arbitrary", "arbitrary")), name="nsa_sample",
    )(page_table.reshape(-1), *([cache_t] * SEL_PAGES), qbd, q_rows, ckcv, new_sel, win_t, win_new, gates)


def _diff_sample_kernel(tn, lam_init, n_steps, pt_ref, *refs):
    pages = refs[:DIFF_PAGES]
    qbd_ref, new_ref, lqk_ref, gsub_ref, o_ref, m_ref, l_ref, acc_ref = refs[DIFF_PAGES:]
    j = pl.program_id(1)
    page_rows = pages[0].shape[0] // DIFF_ROWS
    past = n_steps * DIFF_PAGES * page_rows
    rph = 2 * tn
    rows = DIFF_HEADS * rph
    ridx = lax.broadcasted_iota(jnp.int32, (rows, 1), 0)
    tok = ridx % tn
    slope = _pow2_neg(2 * (ridx // rph + 1)) * LOG2E

    @pl.when(j == 0)
    def _():
        m_ref[...] = jnp.full(m_ref.shape, NEG, F32)
        l_ref[...] = jnp.zeros(l_ref.shape, F32)
        acc_ref[...] = jnp.zeros(acc_ref.shape, F32)

    def attend(key_sets, krel, extra_valid):
        sc = jnp.concatenate(
            [jnp.concatenate([_dot_nt(qbd_ref[0, hd_i], ks(hd_i, 0).astype(BF16)) for ks in key_sets], axis=1)
             for hd_i in range(DIFF_HEADS)], axis=0)
        valid = (krel <= tok) & extra_valid
        sc = jnp.where(valid, sc + slope * krel.astype(F32), NEG)
        m_old = m_ref[...]
        m_new = jnp.maximum(m_old, jnp.max(sc, axis=-1, keepdims=True))
        pr = jnp.where(valid, jnp.exp2(sc - m_new), 0.0)
        alpha = jnp.exp2(m_old - m_new)
        m_ref[...] = m_new
        l_ref[...] = alpha * l_ref[...] + jnp.sum(pr, axis=-1, keepdims=True)
        pr = pr.astype(BF16)
        n = sc.shape[1] // len(key_sets)
        pvs = []
        for hd_i in range(DIFF_HEADS):
            pv = jnp.zeros((rph, DIFF_VD), F32)
            for i, ks in enumerate(key_sets):
                pv = pv + _dot(pr[hd_i * rph:(hd_i + 1) * rph, i * n:(i + 1) * n], ks(hd_i, 1).astype(BF16))
            pvs.append(pv)
        acc_ref[...] = alpha * acc_ref[...] + jnp.concatenate(pvs, axis=0)

    span = DIFF_PAGES * page_rows
    krel = j * span - past + lax.broadcasted_iota(jnp.int32, (1, span), 1)
    attend([lambda hd_i, kind, pg=pg: pg[pl.ds(kind * DIFF_HEADS + hd_i, page_rows, stride=DIFF_ROWS), :]
            for pg in pages], krel, True)

    @pl.when(j == n_steps - 1)
    def _():
        lane = lax.broadcasted_iota(jnp.int32, (1, LANES), 1)
        new_rows = lambda hd_i, kind: _pad_rows(
            new_ref[0, pl.ds(kind * DIFF_HEADS + hd_i, tn, stride=DIFF_ROWS), :], LANES)
        attend([new_rows], lane, lane < tn)
        o = acc_ref[...] / jnp.maximum(l_ref[...], 1e-30)
        lam = _diff_lambda(lqk_ref, lam_init)
        for hd_i in range(DIFF_HEADS):
            o1 = o[hd_i * rph:hd_i * rph + tn]
            o2 = o[hd_i * rph + tn:(hd_i + 1) * rph]
            od = _rms(o1 - lam * o2, gsub_ref[...]) * (1.0 - lam_init)
            o_ref[0, :, hd_i * DIFF_VD:(hd_i + 1) * DIFF_VD] = od


def _diff_sample(page_table, cache_r, qbd, new_diff, lqk, gsub, lam_init):
    bd, pps = page_table.shape
    tn = new_diff.shape[1] // DIFF_ROWS
    rows = 2 * DIFF_HEADS * tn
    n_steps = pps // DIFF_PAGES
    per_b = lambda shape: pl.BlockSpec((1,) + shape, lambda b, j, pt: (b,) + (0,) * len(shape))
    in_specs = _page_specs(DIFF_PAGES, cache_r.shape[1], 0, pps) + [
        per_b(qbd.shape[1:]), per_b(new_diff.shape[1:]),
        pl.BlockSpec(lqk.shape, lambda b, j, pt: (0, 0)),
        pl.BlockSpec(gsub.shape, lambda b, j, pt: (0, 0))]
    grid_spec = pltpu.PrefetchScalarGridSpec(
        num_scalar_prefetch=1, grid=(bd, n_steps), in_specs=in_specs,
        out_specs=per_b((tn, DV_W)),
        scratch_shapes=[pltpu.VMEM((rows, 1), F32), pltpu.VMEM((rows, 1), F32),
                        pltpu.VMEM((rows, DIFF_VD), F32)])
    return pl.pallas_call(
        functools.partial(_diff_sample_kernel, tn, lam_init, n_steps), grid_spec=grid_spec,
        out_shape=jax.ShapeDtypeStruct((bd, tn, DV_W), F32),
        compiler_params=_params(("arbitrary", "arbitrary")), name="diff_sample",
    )(page_table.reshape(-1), *([cache_r] * DIFF_PAGES), qbd, new_diff, lqk, gsub)


def _block_diag2(a):
    z = jnp.zeros_like(a)
    return jnp.concatenate([jnp.concatenate([a, z], axis=-1), jnp.concatenate([z, a], axis=-1)], axis=-2)


def _layer_weights(l, g_ffn1_pre, ffn1_w1, ffn1_w3, ffn1_w2, g_ffn1_post, g_mix_pre, w_in, cmp_w1,
                   cmp_b1, cmp_w2, lambda_qk, g_diff_subln, w_up_nsa, w_up_diff, w_out, g_mix_post,
                   g_ffn2_pre, ffn2_w1, ffn2_w3, ffn2_w2, g_ffn2_post):
    assert NSA_KV_HEADS == 2 and 2 * NSA_BLOCK == LANES
    row = lambda g: g[l][None, :]
    b16 = lambda w: w[l].astype(BF16)
    wi = w_in[l]
    o_ng = NQ_W + NKV_W
    o_dq = o_ng + NG_W
    o_dv = o_dq + 2 * DQ_W
    o_mg = o_dv + DV_W
    ng_cols = jnp.concatenate([wi[:, o_ng:o_dq], jnp.zeros((wi.shape[0], LANES - NG_W), wi.dtype)], axis=1)
    wp = jnp.concatenate([wi[:, :o_ng], wi[:, o_dq:o_mg], ng_cols], axis=1).astype(BF16)
    wt = jnp.concatenate([wi[:, NQ_W:o_ng], wi[:, o_dv:o_mg], ng_cols], axis=1).T.astype(BF16)
    w1c = cmp_w1[l]
    return dict(
        ffn1=(row(g_ffn1_pre), b16(ffn1_w1), b16(ffn1_w3), b16(ffn1_w2), row(g_ffn1_post)),
        ffn2=(row(g_ffn2_pre), b16(ffn2_w1), b16(ffn2_w3), b16(ffn2_w2), row(g_ffn2_post)),
        gmix=row(g_mix_pre), wp=wp, wt=wt, wmg=wi[:, o_mg:].astype(BF16),
        w1r=w1c.reshape(2, NSA_BLOCK * HD, HD).astype(BF16), b1=cmp_b1[l][:, None, :],
        w2c=cmp_w2[l].astype(BF16),
        wd=_block_diag2(w1c.transpose(0, 2, 1, 3)).reshape(2, HD * LANES, LANES).astype(BF16),
        bd2=_block_diag2(cmp_w2[l]).astype(BF16),
        b1t=jnp.tile(cmp_b1[l], (1, 2))[:, None, :],
        lqk=lambda_qk[l], gsub=row(g_diff_subln), wun=b16(w_up_nsa), wud=b16(w_up_diff),
        wout=b16(w_out), gmixpost=row(g_mix_post))


def _prompt_layer(x, W, lam_init):
    bsz, t, d = x.shape
    n = bsz * t
    (x1, q_sep, dq, diff_new, cmp_sep, nsa_new_t, win_new_t, k_rows, v_t, dk, dv_t, ng_t) = _ffn_proj(
        x.reshape(n, d), *W["ffn1"], W["gmix"], W["wp"], W["wt"], tm=512, seq_len=t)
    nb = t // NSA_BLOCK
    xc = cmp_sep.reshape(2 * NSA_KV_HEADS, n // NSA_BLOCK, NSA_BLOCK * HD)
    ckcv = _compress_prompt(xc, W["w1r"], W["b1"], W["w2c"])
    ck = ckcv.reshape(2 * NSA_KV_HEADS, bsz, nb, HD).transpose(1, 0, 2, 3)
    o_nsa = _nsa_prompt(q_sep, ck, ck.transpose(0, 1, 3, 2), k_rows, v_t, ng_t, bsz, t)
    o_diff = _diff_prompt(dq, dk, dv_t, W["lqk"], W["gsub"], lam_init, bsz, t)
    y = _merge_ffn(x1, o_nsa, o_diff, W["gmix"], W["wmg"], W["wun"], W["wud"], W["wout"],
                   W["gmixpost"], *W["ffn2"], tm=512)
    w_rows = min(NSA_WINDOW, t)
    new_nsa = nsa_new_t.reshape(bsz, 4, NSA_KV_HEADS, HD, t).transpose(0, 4, 1, 2, 3)
    new_win = win_new_t[:, :, t - w_rows:].reshape(bsz, 2, NSA_KV_HEADS, HD, w_rows).transpose(0, 4, 1, 2, 3)
    new_diff = diff_new.reshape(bsz, t, 2, DIFF_HEADS, 2 * HD)
    return y.reshape(bsz, t, d), (new_nsa, new_diff, new_win)


def _sample_layer(x, c_nsa, c_diff, win, page_table, W, lam_init):
    bd, tn, d = x.shape
    n = bd * tn
    n_pool, page_rows = c_nsa.shape[:2]
    pps = page_table.shape[1]
    past = pps * page_rows
    nb = -(-(past + tn) // NSA_BLOCK)
    assert page_rows == LANES and nb <= CK_ROWS
    (x1, q_sep, dq, diff_new, nsa_new, win_new, ng) = _ffn_proj(
        x.reshape(n, d), *W["ffn1"], W["gmix"], W["wp"], W["wt"], tm=n)
    nsa_t = c_nsa.transpose(0, 2, 3, 4, 1).reshape(n_pool, NSA_NEW_W, page_rows)
    diff_r = c_diff.reshape(n_pool, page_rows * DIFF_ROWS, LANES)
    win_t = win.transpose(0, 2, 3, 4, 1).reshape(bd, WIN_W, win.shape[1])
    nsa_new3 = nsa_new.reshape(bd, tn, NSA_NEW_W)

    new_cmp_t = jnp.pad(nsa_new3[:, :, :CMP_W].transpose(0, 2, 1), ((0, 0), (0, 0), (0, LANES - tn)))
    ck_pages = _compress_sample(page_table, nsa_t, new_cmp_t, W["wd"], W["b1t"], W["bd2"])
    slots = ck_pages.shape[2]
    ckcv = ck_pages.reshape(bd, 2 * NSA_KV_HEADS, slots * 2, HD)[:, :, :nb]
    ckcv = jnp.pad(ckcv, ((0, 0), (0, 0), (0, CK_ROWS - nb), (0, 0)))

    q_rows = q_sep.reshape(NSA_HEADS, bd, tn, HD).transpose(1, 0, 2, 3).reshape(bd, NSA_HEADS * tn, HD)
    kv_of_row = jnp.arange(NSA_HEADS * tn) // (NSA_GROUP * tn)
    qbd_n = jnp.concatenate(
        [jnp.where((kv_of_row == kvh)[None, :, None], q_rows, 0) for kvh in range(NSA_KV_HEADS)], axis=-1)
    gates = ng[:, :NG_W].reshape(bd, tn, NSA_HEADS, 3).transpose(0, 2, 1, 3).reshape(bd, NSA_HEADS * tn, 3)
    gates = jnp.pad(gates, ((0, 0), (0, 0), (0, LANES - 3)))
    o_nsa = _nsa_sample(page_table, nsa_t, qbd_n, q_rows, ckcv, nsa_new3[:, :, CMP_W:], win_t,
                        win_new.reshape(bd, tn, WIN_W), gates, nb)

    dq5 = dq.reshape(DIFF_HEADS, 2, bd, tn, HD).transpose(2, 0, 1, 3, 4)
    z = jnp.zeros_like(dq5[:, :, 0])
    qbd_d = jnp.concatenate([jnp.concatenate([dq5[:, :, 0], z], axis=-1),
                             jnp.concatenate([z, dq5[:, :, 1]], axis=-1)], axis=2)
    o_diff = _diff_sample(page_table, diff_r, qbd_d, diff_new.reshape(bd, tn * DIFF_ROWS, LANES),
                          W["lqk"], W["gsub"], lam_init)
    y = _merge_ffn(x1, o_nsa.reshape(n, NQ_W), o_diff.reshape(n, DV_W), W["gmix"], W["wmg"],
                   W["wun"], W["wud"], W["wout"], W["gmixpost"], *W["ffn2"], tm=n)
    new_nsa = nsa_new.reshape(bd, tn, 4, NSA_KV_HEADS, HD)
    new_diff = diff_new.reshape(bd, tn, 2, DIFF_HEADS, 2 * HD)
    new_win = jnp.concatenate([win, win_new.reshape(bd, tn, 2, NSA_KV_HEADS, HD)], axis=1)[:, tn:]
    return y.reshape(bd, tn, d), (new_nsa, new_diff, new_win)


def kernel(x_prompt, x_sample, cache_nsa_kv, cache_diff_kv, state_win_kv, page_table, g_ffn1_pre, ffn1_w1, ffn1_w3, ffn1_w2, g_ffn1_post, g_mix_pre, w_in, cmp_w1, cmp_b1, cmp_w2, lambda_qk, g_diff_subln, w_up_nsa, w_up_diff, w_out, g_mix_post, g_ffn2_pre, ffn2_w1, ffn2_w3, ffn2_w2, g_ffn2_post):
    depth = w_in.shape[0]
    y_p, y_s = x_prompt, x_sample
    outs = [[] for _ in range(6)]
    for l in range(depth):
        lam_init = 0.8 - 0.6 * math.exp(-0.3 * l)
        W = _layer_weights(l, g_ffn1_pre, ffn1_w1, ffn1_w3, ffn1_w2, g_ffn1_post, g_mix_pre, w_in,
                           cmp_w1, cmp_b1, cmp_w2, lambda_qk, g_diff_subln, w_up_nsa, w_up_diff,
                           w_out, g_mix_post, g_ffn2_pre, ffn2_w1, ffn2_w3, ffn2_w2, g_ffn2_post)
        y_p, (a_p, b_p, c_p) = _prompt_layer(y_p, W, lam_init)
        y_s, (a_s, b_s, c_s) = _sample_layer(y_s, cache_nsa_kv[l], cache_diff_kv[l], state_win_kv[l],
                                             page_table, W, lam_init)
        for lst, val in zip(outs, (a_p, a_s, b_p, b_s, c_p, c_s)):
            lst.append(val)
    return (y_p, y_s) + tuple(jnp.stack(o) for o in outs)
```

```python
import functools
import math

import jax
import jax.numpy as jnp
from jax import lax
from jax.experimental import pallas as pl
from jax.experimental.pallas import tpu as pltpu

F32 = jnp.float32
BF16 = jnp.bfloat16

HD = 64
NSA_HEADS = 8
NSA_KV_HEADS = 2
NSA_GROUP = NSA_HEADS // NSA_KV_HEADS
NSA_BLOCK = 64
NSA_TOPN = 16
NSA_WINDOW = 512
DIFF_HEADS = 4
DIFF_VD = 2 * HD
EPS = 1e-6
FORCE_SCORE = 1e4
NEG = -1e30
LOG2E = 1.0 / math.log(2.0)
Q_SCALE = HD ** -0.5 * LOG2E
LANES = 128
SUBLANES = 8
FF_CHUNK = 256
VMEM_LIMIT = 56 * 1024 * 1024

NQ_W = NSA_HEADS * HD
NKV_W = 6 * NSA_KV_HEADS * HD
NSA_NEW_W = 4 * NSA_KV_HEADS * HD
CMP_W = 2 * NSA_KV_HEADS * HD
WIN_W = 2 * NSA_KV_HEADS * HD
NG_W = 3 * NSA_HEADS
DQ_W = DIFF_HEADS * 2 * HD
DV_W = DIFF_HEADS * DIFF_VD
DIFF_ROWS = 2 * DIFF_HEADS


def _nsa_slope2(head):
    return 2.0 ** (-8.0 * (head + 1) / NSA_HEADS) * LOG2E


def _dot(a, b):
    return jnp.dot(a, b, preferred_element_type=F32)


def _dot_nt(a, b):
    return lax.dot_general(a, b, (((1,), (1,)), ((), ())), preferred_element_type=F32)


def _rms(x, g):
    return x * lax.rsqrt(jnp.mean(x * x, axis=-1, keepdims=True) + EPS) * g


def _gelu_tanh(x):
    c = math.sqrt(2.0 / math.pi)
    return 0.5 * x * (1.0 + jnp.tanh(c * (x + 0.044715 * (x * x * x))))


def _const_spec(shape):
    nd = len(shape)
    return pl.BlockSpec(shape, lambda *_: (0,) * nd, pipeline_mode=pl.Buffered(1))


def _params(sem):
    return pltpu.CompilerParams(dimension_semantics=sem, vmem_limit_bytes=VMEM_LIMIT)


def _softmax2(s, valid, axis=-1):
    s = jnp.where(valid, s, NEG)
    p = jnp.where(valid, jnp.exp2(s - jnp.max(s, axis=axis, keepdims=True)), 0.0)
    return p / jnp.maximum(jnp.sum(p, axis=axis, keepdims=True), 1e-30)


def _ffn_half_step(x, g_pre, w1_ref, w3_ref, w2_ref, g_post):
    xn = _rms(x, g_pre).astype(BF16)
    d_ff = w1_ref.shape[1]
    acc = jnp.zeros(x.shape, F32)
    for c in range(d_ff // FF_CHUNK):
        cs = slice(c * FF_CHUNK, (c + 1) * FF_CHUNK)
        h1 = _dot(xn, w1_ref[:, cs])
        h3 = _dot(xn, w3_ref[:, cs])
        gate = (h1 * jax.nn.sigmoid(h1) * h3).astype(BF16)
        acc = acc + _dot(gate, w2_ref[cs, :])
    return x + 0.5 * _rms(acc, g_post)


def _store_heads(ref, x, width, scale=None):
    for i in range(ref.shape[0]):
        piece = x[:, i * width:(i + 1) * width]
        ref[i] = (piece if scale is None else piece * scale).astype(ref.dtype)


def _store_diff_rows(ref, dk, dv, tm):
    for i in range(DIFF_HEADS):
        ref[pl.ds(i, tm, stride=DIFF_ROWS), :] = dk[:, i * 2 * HD:(i + 1) * 2 * HD]
        ref[pl.ds(DIFF_HEADS + i, tm, stride=DIFF_ROWS), :] = dv[:, i * DIFF_VD:(i + 1) * DIFF_VD]


def _ffn_proj_kernel(prompt, x_ref, gpre_ref, w1_ref, w3_ref, w2_ref, gpost_ref, gmix_ref,
                     wp_ref, wt_ref, x1_ref, q_ref, dq_ref, diff_new_ref, *out_refs):
    tm = x_ref.shape[0]
    x1 = _ffn_half_step(x_ref[...], gpre_ref[...], w1_ref, w3_ref, w2_ref, gpost_ref[...])
    x1_ref[...] = x1
    h = _rms(x1, gmix_ref[...]).astype(BF16)
    o_nkv = NQ_W
    o_dq = o_nkv + NKV_W
    o_dk = o_dq + DQ_W
    o_dv = o_dk + DQ_W
    o_ng = o_dv + DV_W
    _store_heads(q_ref, _dot(h, wp_ref[:, :NQ_W]), HD, Q_SCALE)
    _store_heads(dq_ref, _dot(h, wp_ref[:, o_dq:o_dk]), HD, Q_SCALE)
    nkv = _dot(h, wp_ref[:, o_nkv:o_dq])
    dk = _dot(h, wp_ref[:, o_dk:o_dv])
    dv = _dot(h, wp_ref[:, o_dv:o_ng])
    _store_diff_rows(diff_new_ref, dk, dv, tm)
    if prompt:
        (cmp_ref, nsa_new_ref, win_new_ref, krow_ref, vt_ref, dk_ref, dvt_ref, ngt_ref) = out_refs
        _store_heads(cmp_ref, nkv[:, :CMP_W], HD)
        _store_heads(dk_ref, dk, HD)
        sel_k = nkv[:, CMP_W:CMP_W + NSA_KV_HEADS * HD]
        win_k = nkv[:, NSA_NEW_W:NSA_NEW_W + NSA_KV_HEADS * HD]
        _store_heads(krow_ref, jnp.concatenate([sel_k, win_k], axis=1), HD)
        zt = _dot_nt(wt_ref[...], h)
        nsa_new_ref[0] = zt[:NSA_NEW_W]
        win_new_ref[0] = zt[NSA_NEW_W:NKV_W]
        for kvh in range(NSA_KV_HEADS):
            r_sel = CMP_W + (NSA_KV_HEADS + kvh) * HD
            r_win = NSA_NEW_W + (NSA_KV_HEADS + kvh) * HD
            vt_ref[kvh] = zt[r_sel:r_sel + HD].astype(BF16)
            vt_ref[NSA_KV_HEADS + kvh] = zt[r_win:r_win + HD].astype(BF16)
        for i in range(DIFF_HEADS):
            dvt_ref[i] = zt[NKV_W + i * DIFF_VD:NKV_W + (i + 1) * DIFF_VD].astype(BF16)
        ngt_ref[0] = jax.nn.sigmoid(zt[NKV_W + DV_W:])
    else:
        (nsa_new_ref, win_new_ref, ng_ref) = out_refs
        nsa_new_ref[...] = nkv[:, :NSA_NEW_W]
        win_new_ref[...] = nkv[:, NSA_NEW_W:]
        ng_ref[...] = jax.nn.sigmoid(_dot(h, wp_ref[:, o_ng:]))


def _ffn_proj(x, gpre, w1, w3, w2, gpost, gmix, wp, wt, tm, seq_len=None):
    n, d = x.shape
    d_ff = w1.shape[1]
    row = lambda w: pl.BlockSpec((tm, w), lambda i: (i, 0))
    sep = lambda k, w: pl.BlockSpec((k, tm, w), lambda i: (0, i, 0))
    sds = jax.ShapeDtypeStruct
    out_shape = [sds((n, d), F32),
                 sds((NSA_HEADS, n, HD), BF16),
                 sds((2 * DIFF_HEADS, n, HD), BF16),
                 sds((n * DIFF_ROWS, LANES), F32)]
    out_specs = [row(d), sep(NSA_HEADS, HD), sep(2 * DIFF_HEADS, HD),
                 pl.BlockSpec((tm * DIFF_ROWS, LANES), lambda i: (i, 0))]
    if seq_len is not None:
        tiles = seq_len // tm
        nseq = n // seq_len
        fmaj = lambda w: pl.BlockSpec((1, w, tm), lambda i: (i // tiles, 0, i % tiles))
        tsep = lambda k, w: pl.BlockSpec((k, w, tm), lambda i: (0, 0, i))
        out_shape += [sds((2 * NSA_KV_HEADS, n, HD), BF16),
                      sds((nseq, NSA_NEW_W, seq_len), F32),
                      sds((nseq, WIN_W, seq_len), F32),
                      sds((2 * NSA_KV_HEADS, n, HD), BF16),
                      sds((2 * NSA_KV_HEADS, HD, n), BF16),
                      sds((2 * DIFF_HEADS, n, HD), BF16),
                      sds((DIFF_HEADS, DIFF_VD, n), BF16),
                      sds((nseq, LANES, seq_len), F32)]
        out_specs += [sep(2 * NSA_KV_HEADS, HD), fmaj(NSA_NEW_W), fmaj(WIN_W), sep(2 * NSA_KV_HEADS, HD),
                      tsep(2 * NSA_KV_HEADS, HD), sep(2 * DIFF_HEADS, HD), tsep(DIFF_HEADS, DIFF_VD),
                      fmaj(LANES)]
    else:
        out_shape += [sds((n, NSA_NEW_W), F32), sds((n, WIN_W), F32), sds((n, LANES), F32)]
        out_specs += [row(NSA_NEW_W), row(WIN_W), row(LANES)]
    in_specs = [row(d), _const_spec((1, d)), _const_spec((d, d_ff)), _const_spec((d, d_ff)),
                _const_spec((d_ff, d)), _const_spec((1, d)), _const_spec((1, d)),
                _const_spec(wp.shape), _const_spec(wt.shape)]
    return pl.pallas_call(
        functools.partial(_ffn_proj_kernel, seq_len is not None), grid=(n // tm,), in_specs=in_specs,
        out_specs=out_specs, out_shape=out_shape, compiler_params=_params(("parallel",)),
        name="ffn_proj",
    )(x, gpre, w1, w3, w2, gpost, gmix, wp, wt)


def _compress_kernel(x_ref, w1_ref, b1_ref, w2_ref, o_ref):
    hid = _gelu_tanh(_dot(x_ref[0], w1_ref[0]) + b1_ref[0])
    o_ref[0] = _dot(hid.astype(BF16), w2_ref[0])


def _compress_prompt(xc, w1r, b1, w2):
    k, nb, kd = xc.shape
    return pl.pallas_call(
        _compress_kernel, grid=(k,),
        in_specs=[pl.BlockSpec((1, nb, kd), lambda i: (i, 0, 0)),
                  pl.BlockSpec((1, kd, HD), lambda i: (i // NSA_KV_HEADS, 0, 0)),
                  pl.BlockSpec((1, 1, HD), lambda i: (i // NSA_KV_HEADS, 0, 0)),
                  pl.BlockSpec((1, HD, HD), lambda i: (i // NSA_KV_HEADS, 0, 0))],
        out_specs=pl.BlockSpec((1, nb, HD), lambda i: (i, 0, 0)),
        out_shape=jax.ShapeDtypeStruct((k, nb, HD), F32),
        compiler_params=_params(("parallel",)), name="nsa_compress_prompt",
    )(xc, w1r, b1, w2)


NSA_TQ = 128
NSA_TK = 512


def _topn_mask_t(imp_t):
    nb, nt = imp_t.shape
    sub = lax.broadcasted_iota(jnp.int32, (SUBLANES, nt), 0)
    out = []
    for j0 in range(0, nb, SUBLANES):
        vals = imp_t[j0:j0 + SUBLANES]
        rank = jnp.zeros((SUBLANES, nt), F32)
        for i in range(nb):
            vi = jnp.broadcast_to(imp_t[i:i + 1, :], (SUBLANES, nt))
            if i < j0:
                ahead = vi >= vals
            elif i >= j0 + SUBLANES:
                ahead = vi > vals
            else:
                ahead = (vi > vals) | ((vi == vals) & (sub > i - j0))
            rank = rank + jnp.where(ahead, 1.0, 0.0)
        out.append(jnp.where((rank < NSA_TOPN) & (vals >= 0.0), 1.0, 0.0))
    return jnp.concatenate(out, axis=0)


def _nsa_prompt_kernel(q_ref, ck_ref, cvt_ref, krow_ref, vt_ref, ngt_ref, o_ref, sel_ref, bias_ref):
    qi = pl.program_id(1)
    s0 = qi * NSA_TQ
    nb = ck_ref.shape[2]
    cols = NSA_GROUP * NSA_TQ
    lane = lax.broadcasted_iota(jnp.int32, (1, cols), 1)
    tok = lane % NSA_TQ
    gidx = lane // NSA_TQ
    tok1 = lax.broadcasted_iota(jnp.int32, (1, NSA_TQ), 1)
    span = NSA_WINDOW + NSA_TQ
    key1 = lax.broadcasted_iota(jnp.int32, (span, NSA_TQ), 0)
    keyf = key1.astype(F32)
    w0 = pl.multiple_of(jnp.maximum(s0 - NSA_WINDOW, 0), NSA_TQ)
    wdist = tok1 - (w0 - s0 + key1)
    wvalid = (wdist >= 0) & (wdist < NSA_WINDOW)
    blk_c = lax.broadcasted_iota(jnp.int32, (nb, cols), 0)
    end_c = (blk_c + 1) * NSA_BLOCK - 1 - s0
    blk_t = lax.broadcasted_iota(jnp.int32, (nb, NSA_TQ), 0)
    cur = (s0 + tok1) // NSA_BLOCK
    gcols = [slice(g * NSA_TQ, (g + 1) * NSA_TQ) for g in range(NSA_GROUP)]
    slopes = [[_nsa_slope2(kvh * NSA_GROUP + g) for g in range(NSA_GROUP)] for kvh in range(NSA_KV_HEADS)]
    kv_heads = range(NSA_KV_HEADS)

    qs, o_cs, sel_ts = [], [], []
    for kvh in kv_heads:
        slope = jnp.zeros((1, cols), F32)
        for g in range(NSA_GROUP):
            slope = jnp.where(gidx == g, slopes[kvh][g], slope)
        q = q_ref[kvh * NSA_GROUP:(kvh + 1) * NSA_GROUP].reshape(cols, HD)
        ck = ck_ref[0, kvh].astype(BF16)
        cv_t = cvt_ref[0, NSA_KV_HEADS + kvh].astype(BF16)
        p_t = _softmax2(_dot_nt(ck, q) + slope * end_c.astype(F32), tok >= end_c, axis=0)
        imp = p_t[:, gcols[0]]
        for g in range(1, NSA_GROUP):
            imp = imp + p_t[:, gcols[g]]
        forced = (blk_t == cur) | (blk_t == 0)
        imp = jnp.where(forced, FORCE_SCORE, jnp.where(blk_t <= cur, imp, -1.0))
        sel_ref[kvh] = _topn_mask_t(imp)
        qs.append(q)
        o_cs.append(_dot(cv_t, p_t.astype(BF16)))
        sel_ts.append(sel_ref[kvh].astype(BF16))
        for g in range(NSA_GROUP):
            bias_ref[kvh, g] = slopes[kvh][g] * keyf[:NSA_TK]

    def sel_tile(kvh, kt, carry, diagonal):
        k0 = pl.multiple_of(kt * NSA_TK, NSA_TK)
        k = krow_ref[kvh, pl.ds(k0, NSA_TK), :]
        v_t = vt_ref[kvh, :, pl.ds(k0, NSA_TK)]
        expand = (lax.broadcasted_iota(jnp.int32, (NSA_TK, nb), 1)
                  == (k0 + lax.broadcasted_iota(jnp.int32, (NSA_TK, nb), 0)) // NSA_BLOCK)
        valid = _dot(expand.astype(BF16), sel_ts[kvh]) > 0.5
        if diagonal:
            valid = valid & (k0 - s0 + key1[:NSA_TK] <= tok1)
        t_all = _dot_nt(k, qs[kvh])
        new, prs = [], []
        for g in range(NSA_GROUP):
            m, l = carry[g]
            t = jnp.where(valid, t_all[:, gcols[g]] + bias_ref[kvh, g], NEG)
            shift = slopes[kvh][g] * (k0 - s0).astype(F32)
            m_new = jnp.maximum(m, jnp.max(t, axis=0, keepdims=True) + shift)
            pr = jnp.exp2(t - (m_new - shift))
            alpha = jnp.exp2(m - m_new)
            new.append((m_new, alpha * l + jnp.sum(pr, axis=0, keepdims=True), alpha))
            prs.append(pr.astype(BF16))
        alpha = jnp.concatenate([n_[2] for n_ in new], axis=1)
        acc = alpha * carry[NSA_GROUP] + _dot(v_t, jnp.concatenate(prs, axis=1))
        return tuple((n_[0], n_[1]) for n_ in new) + (acc,)

    def both_tiles(kt, carry, diagonal):
        return tuple(sel_tile(kvh, kt, carry[kvh], diagonal) for kvh in kv_heads)

    def maybe_tiles(kt, carry):
        blocks = NSA_TK // NSA_BLOCK
        used = jnp.max(sel_ref[:, pl.ds(pl.multiple_of(kt * blocks, blocks), blocks), :]) > 0.0
        return lax.cond(used, lambda c: both_tiles(kt, c, False), lambda c: c, carry)

    n_full = s0 // NSA_TK
    stat = (jnp.full((1, NSA_TQ), NEG, F32), jnp.zeros((1, NSA_TQ), F32))
    init = (stat,) * NSA_GROUP + (jnp.zeros((HD, cols), F32),)
    carry = both_tiles(n_full, lax.fori_loop(0, n_full, maybe_tiles, (init,) * NSA_KV_HEADS), True)

    for kvh in kv_heads:
        l_s = jnp.concatenate([carry[kvh][g][1] for g in range(NSA_GROUP)], axis=1)
        o_s = carry[kvh][NSA_GROUP] / jnp.maximum(l_s, 1e-30)

        kw = krow_ref[NSA_KV_HEADS + kvh, pl.ds(w0, span), :]
        vw_t = vt_ref[NSA_KV_HEADS + kvh, :, pl.ds(w0, span)]
        tw = _dot_nt(kw, qs[kvh])
        pws, lws = [], []
        for g in range(NSA_GROUP):
            t = jnp.where(wvalid, tw[:, gcols[g]] + slopes[kvh][g] * keyf, NEG)
            pw = jnp.exp2(t - jnp.max(t, axis=0, keepdims=True))
            lws.append(jnp.sum(pw, axis=0, keepdims=True))
            pws.append(pw.astype(BF16))
        o_w = _dot(vw_t, jnp.concatenate(pws, axis=1)) / jnp.concatenate(lws, axis=1)

        gated = []
        for g in range(NSA_GROUP):
            row = 3 * (kvh * NSA_GROUP + g)
            gated.append(ngt_ref[0, row:row + 1, :] * o_cs[kvh][:, gcols[g]]
                         + ngt_ref[0, row + 1:row + 2, :] * o_s[:, gcols[g]]
                         + ngt_ref[0, row + 2:row + 3, :] * o_w[:, gcols[g]])
        for g in range(0, NSA_GROUP, 2):
            c0 = (kvh * NSA_GROUP + g) * HD
            pair = jnp.concatenate([gated[g], gated[g + 1]], axis=0)
            o_ref[:, c0:c0 + 2 * HD] = pair.T.astype(o_ref.dtype)


def _nsa_prompt(q_sep, ck, cv_t, k_rows, v_t, ng_t, bsz, t):
    nq = t // NSA_TQ
    nb = t // NSA_BLOCK
    n = bsz * t
    return pl.pallas_call(
        _nsa_prompt_kernel, grid=(bsz, nq),
        in_specs=[pl.BlockSpec((NSA_HEADS, NSA_TQ, HD), lambda b, i: (0, b * nq + i, 0)),
                  pl.BlockSpec((1, 2 * NSA_KV_HEADS, nb, HD), lambda b, i: (b, 0, 0, 0)),
                  pl.BlockSpec((1, 2 * NSA_KV_HEADS, HD, nb), lambda b, i: (b, 0, 0, 0)),
                  pl.BlockSpec((2 * NSA_KV_HEADS, t, HD), lambda b, i: (0, b, 0)),
                  pl.BlockSpec((2 * NSA_KV_HEADS, HD, t), lambda b, i: (0, 0, b)),
                  pl.BlockSpec((1, LANES, NSA_TQ), lambda b, i: (b, 0, i))],
        out_specs=pl.BlockSpec((NSA_TQ, NQ_W), lambda b, i: (b * nq + i, 0)),
        out_shape=jax.ShapeDtypeStruct((n, NQ_W), BF16),
        scratch_shapes=[pltpu.VMEM((NSA_KV_HEADS, nb, NSA_TQ), F32),
                        pltpu.VMEM((NSA_KV_HEADS, NSA_GROUP, NSA_TK, NSA_TQ), F32)],
        compiler_params=_params(("parallel", "parallel")), name="nsa_prompt",
    )(q_sep, ck, cv_t, k_rows, v_t, ng_t)


DIFF_T = 512


def _diff_lambda(lqk_ref, lam_init):
    l = lqk_ref[...]
    a = jnp.sum(l[0:1] * l[1:2], axis=-1, keepdims=True)
    b = jnp.sum(l[2:3] * l[3:4], axis=-1, keepdims=True)
    return jnp.exp(a) - jnp.exp(b) + lam_init


def _pow2_neg(n_vec):
    return lax.bitcast_convert_type((127 - n_vec) << 23, F32)


def _diff_prompt_kernel(lam_init, q_ref, k_ref, vt_ref, lqk_ref, gsub_ref, o_ref):
    h = pl.program_id(1)
    qi = pl.program_id(2)
    s0 = qi * DIFF_T
    slope = _pow2_neg(jnp.full((1, 1), 2 * (h + 1), jnp.int32)) * LOG2E
    key_i = lax.broadcasted_iota(jnp.int32, (DIFF_T, DIFF_T), 0)
    causal = key_i <= lax.broadcasted_iota(jnp.int32, (DIFF_T, DIFF_T), 1)
    bias0 = slope * key_i.astype(F32)
    q1 = q_ref[0]
    q2 = q_ref[1]

    def tile(kt, carry, diagonal):
        k0 = pl.multiple_of(kt * DIFF_T, DIFF_T)
        shift = slope * (k0 - s0).astype(F32)
        v_t = vt_ref[0, :, pl.ds(k0, DIFF_T)]

        def one(q, k, state):
            m, l, acc = state
            t = _dot_nt(k, q) + bias0
            if diagonal:
                t = jnp.where(causal, t, NEG)
            m_new = jnp.maximum(m, jnp.max(t, axis=0, keepdims=True) + shift)
            pr = jnp.exp2(t - (m_new - shift))
            alpha = jnp.exp2(m - m_new)
            l = alpha * l + jnp.sum(pr, axis=0, keepdims=True)
            return m_new, l, alpha * acc + _dot(v_t, pr.astype(BF16))

        return (one(q1, k_ref[0, pl.ds(k0, DIFF_T), :], carry[0]),
                one(q2, k_ref[1, pl.ds(k0, DIFF_T), :], carry[1]))

    state0 = (jnp.full((1, DIFF_T), NEG, F32), jnp.zeros((1, DIFF_T), F32),
              jnp.zeros((DIFF_VD, DIFF_T), F32))
    carry = lax.fori_loop(0, qi, functools.partial(tile, diagonal=False), (state0, state0))
    (_, l1, acc1), (_, l2, acc2) = tile(qi, carry, True)
    lam = _diff_lambda(lqk_ref, lam_init)
    o = (acc1 / l1 - lam * (acc2 / l2)).T
    o_ref[...] = (_rms(o, gsub_ref[...]) * (1.0 - lam_init)).astype(o_ref.dtype)


def _diff_prompt(dq, dk, dv_t, lqk, gsub, lam_init, bsz, t):
    nq = t // DIFF_T
    n = bsz * t
    return pl.pallas_call(
        functools.partial(_diff_prompt_kernel, lam_init), grid=(bsz, DIFF_HEADS, nq),
        in_specs=[pl.BlockSpec((2, DIFF_T, HD), lambda b, h, i: (h, b * nq + i, 0)),
                  pl.BlockSpec((2, t, HD), lambda b, h, i: (h, b, 0)),
                  pl.BlockSpec((1, DIFF_VD, t), lambda b, h, i: (h, 0, b)),
                  pl.BlockSpec(lqk.shape, lambda b, h, i: (0, 0)),
                  pl.BlockSpec(gsub.shape, lambda b, h, i: (0, 0))],
        out_specs=pl.BlockSpec((DIFF_T, DIFF_VD), lambda b, h, i: (b * nq + i, h)),
        out_shape=jax.ShapeDtypeStruct((n, DV_W), BF16),
        compiler_params=_params(("parallel", "parallel", "parallel")), name="diff_prompt",
    )(dq, dk, dv_t, lqk, gsub)


def _merge_ffn_kernel(x_ref, on_ref, od_ref, gmix_ref, wmg_ref, wun_ref, wud_ref, wout_ref,
                      gmixpost_ref, gpre_ref, w1_ref, w3_ref, w2_ref, gpost_ref, y_ref):
    x1 = x_ref[...]
    d = x1.shape[1]
    h = _rms(x1, gmix_ref[...]).astype(BF16)
    up_n = _dot(on_ref[...].astype(BF16), wun_ref[...])
    up_d = _dot(od_ref[...].astype(BF16), wud_ref[...])
    m = (jax.nn.sigmoid(_dot(h, wmg_ref[:, :d])) * up_n
         + jax.nn.sigmoid(_dot(h, wmg_ref[:, d:])) * up_d)
    out = _dot(m.astype(BF16), wout_ref[...])
    x2 = x1 + _rms(out, gmixpost_ref[...])
    y_ref[...] = _ffn_half_step(x2, gpre_ref[...], w1_ref, w3_ref, w2_ref, gpost_ref[...])


def _merge_ffn(x1, o_nsa, o_diff, gmix, wmg, wun, wud, wout, gmixpost, gpre, w1, w3, w2, gpost, tm):
    n, d = x1.shape
    d_ff = w1.shape[1]
    row = lambda w: pl.BlockSpec((tm, w), lambda i: (i, 0))
    in_specs = [row(d), row(NQ_W), row(DV_W), _const_spec((1, d)), _const_spec(wmg.shape),
                _const_spec(wun.shape), _const_spec(wud.shape), _const_spec(wout.shape),
                _const_spec((1, d)), _const_spec((1, d)), _const_spec((d, d_ff)),
                _const_spec((d, d_ff)), _const_spec((d_ff, d)), _const_spec((1, d))]
    return pl.pallas_call(
        _merge_ffn_kernel, grid=(n // tm,), in_specs=in_specs, out_specs=row(d),
        out_shape=jax.ShapeDtypeStruct((n, d), F32),
        compiler_params=_params(("parallel",)), name="merge_ffn",
    )(x1, o_nsa, o_diff, gmix, wmg, wun, wud, wout, gmixpost, gpre, w1, w3, w2, gpost)


CMP_PAGES = 16
SEL_PAGES = 16
DIFF_PAGES = 16
CK_ROWS = 384


def _page_specs(n_pages, block_rows, half, pages_per_seq):
    def make(i):
        def imap(b, j, pt):
            return (pt[b * pages_per_seq + j * n_pages + i], half, 0)
        return pl.BlockSpec((None, block_rows, LANES), imap)
    return [make(i) for i in range(n_pages)]


def _compress_sample_kernel(n_steps, slots, pt_ref, *refs):
    pages = refs[:CMP_PAGES]
    new_ref, wd_ref, b1_ref, w2_ref, o_ref, xs_ref = refs[CMP_PAGES:]
    j = pl.program_id(1)
    n_pages = n_steps * CMP_PAGES

    @pl.when(j == 0)
    def _():
        xs_ref[pl.ds(n_pages * CMP_W, CMP_W), :] = new_ref[0]
        xs_ref[pl.ds((n_pages + 1) * CMP_W, (slots - n_pages - 1) * CMP_W), :] = jnp.zeros(
            ((slots - n_pages - 1) * CMP_W, LANES), F32)

    for i, pg in enumerate(pages):
        r0 = pl.multiple_of((j * CMP_PAGES + i) * CMP_W, CMP_W)
        xs_ref[pl.ds(r0, CMP_W), :] = pg[...]

    @pl.when(j == n_steps - 1)
    def _():
        for c in range(2 * NSA_KV_HEADS):
            kind = c // NSA_KV_HEADS
            lhs = jnp.concatenate(
                [xs_ref[pl.ds(c * HD + d, slots, stride=CMP_W), :].astype(BF16) for d in range(HD)],
                axis=1)
            hid = _gelu_tanh(_dot(lhs, wd_ref[kind]) + b1_ref[kind]).astype(BF16)
            o_ref[0, c] = _dot(hid, w2_ref[kind])


def _compress_sample(page_table, cache_t, new_cmp_t, wd, b1t, bd2):
    bd, pps = page_table.shape
    slots = -(-(pps + 1) // SUBLANES) * SUBLANES
    n_steps = pps // CMP_PAGES
    in_specs = _page_specs(CMP_PAGES, CMP_W, 0, pps) + [
        pl.BlockSpec((1, CMP_W, LANES), lambda b, j, pt: (b, 0, 0)),
        pl.BlockSpec(wd.shape, lambda b, j, pt: (0, 0, 0), pipeline_mode=pl.Buffered(1)),
        pl.BlockSpec(b1t.shape, lambda b, j, pt: (0, 0, 0), pipeline_mode=pl.Buffered(1)),
        pl.BlockSpec(bd2.shape, lambda b, j, pt: (0, 0, 0), pipeline_mode=pl.Buffered(1))]
    grid_spec = pltpu.PrefetchScalarGridSpec(
        num_scalar_prefetch=1, grid=(bd, n_steps), in_specs=in_specs,
        out_specs=pl.BlockSpec((1, 2 * NSA_KV_HEADS, slots, LANES), lambda b, j, pt: (b, 0, 0, 0)),
        scratch_shapes=[pltpu.VMEM((slots * CMP_W, LANES), F32)])
    return pl.pallas_call(
        functools.partial(_compress_sample_kernel, n_steps, slots), grid_spec=grid_spec,
        out_shape=jax.ShapeDtypeStruct((bd, 2 * NSA_KV_HEADS, slots, LANES), F32),
        compiler_params=_params(("arbitrary", "arbitrary")), name="nsa_compress_sample",
    )(page_table.reshape(-1), *([cache_t] * CMP_PAGES), new_cmp_t, wd, b1t, bd2)


def _head_diag(full, n_heads, rows_per_head, width):
    return jnp.concatenate(
        [full[h * rows_per_head:(h + 1) * rows_per_head, h * width:(h + 1) * width]
         for h in range(n_heads)], axis=0)


def _pad_rows(x, n):
    return jnp.concatenate([x, jnp.zeros((n - x.shape[0], x.shape[1]), x.dtype)], axis=0)


def _nsa_sample_kernel(tn, nb, n_steps, pt_ref, *refs):
    pages = refs[:SEL_PAGES]
    (qbd_ref, q_ref, ckcv_ref, new_ref, win_ref, wnew_ref, gate_ref, o_ref,
     sel_ref, oc_ref, m_ref, l_ref, acc_ref) = refs[SEL_PAGES:]
    j = pl.program_id(1)
    page_rows = pages[0].shape[1]
    past = n_steps * SEL_PAGES * page_rows
    rows = NSA_HEADS * tn
    rpk = NSA_GROUP * tn
    kvw = NSA_KV_HEADS * HD
    ridx = lax.broadcasted_iota(jnp.int32, (rows, 1), 0)
    tok = ridx % tn
    head = ridx // tn
    slope = jnp.zeros((rows, 1), F32)
    for hd_i in range(NSA_HEADS):
        slope = jnp.where(head == hd_i, _nsa_slope2(hd_i), slope)
    qbd = qbd_ref[0]

    @pl.when(j == 0)
    def _():
        blk = lax.broadcasted_iota(jnp.int32, (1, CK_ROWS), 1)
        end_c = (blk + 1) * NSA_BLOCK - 1 - past
        imps, ocs = [], []
        for kvh in range(NSA_KV_HEADS):
            r = slice(kvh * rpk, (kvh + 1) * rpk)
            ck = ckcv_ref[0, kvh].astype(BF16)
            cv = ckcv_ref[0, NSA_KV_HEADS + kvh].astype(BF16)
            p = _softmax2(_dot_nt(q_ref[0, r], ck) + slope[r] * end_c.astype(F32),
                          (tok[r] >= end_c) & (blk < nb))
            ocs.append(_dot(p.astype(BF16), cv))
            imp = p[0:tn]
            for g in range(1, NSA_GROUP):
                imp = imp + p[g * tn:(g + 1) * tn]
            imps.append(imp)
        oc_ref[...] = jnp.concatenate(ocs, axis=0)
        imp = jnp.concatenate(imps, axis=0)
        tq = past + lax.broadcasted_iota(jnp.int32, (NSA_KV_HEADS * tn, 1), 0) % tn
        cur = tq // NSA_BLOCK
        forced = (blk == cur) | (blk == 0)
        imp = jnp.where(forced, FORCE_SCORE, jnp.where(blk <= cur, imp, -1.0))
        imp = jnp.where(blk < nb, imp, -3.0)

        def pick(_, carry):
            vals, sel = carry
            top = jnp.max(vals, axis=-1, keepdims=True)
            first = jnp.min(jnp.where(vals == top, blk, CK_ROWS), axis=-1, keepdims=True)
            hit = blk == first
            sel = jnp.where(hit & (top >= 0.0), 1.0, sel)
            return jnp.where(hit, -2.0, vals), sel

        _, sel = lax.fori_loop(0, min(NSA_TOPN, nb), pick, (imp, jnp.zeros(imp.shape, F32)))
        sel_rows = jnp.concatenate(
            [sel[kvh * tn:(kvh + 1) * tn] for kvh in range(NSA_KV_HEADS) for _ in range(NSA_GROUP)],
            axis=0)
        for tile in range(CK_ROWS // LANES):
            sel_ref[tile] = sel_rows[:, tile * LANES:(tile + 1) * LANES]
        m_ref[...] = jnp.full(m_ref.shape, NEG, F32)
        l_ref[...] = jnp.zeros(l_ref.shape, F32)
        acc_ref[...] = jnp.zeros(acc_ref.shape, F32)

    def attend(scores, values, krel, blk_rel, sel_tile, extra_valid):
        n = scores.shape[1]
        expand = (lax.broadcasted_iota(jnp.int32, (LANES, n), 0) == blk_rel).astype(BF16)
        hit = _dot(sel_tile.astype(BF16), expand)
        valid = (hit > 0.5) & (krel <= tok) & extra_valid
        sc = jnp.where(valid, scores + slope * krel.astype(F32), NEG)
        m_old = m_ref[...]
        m_new = jnp.maximum(m_old, jnp.max(sc, axis=-1, keepdims=True))
        pr = jnp.where(valid, jnp.exp2(sc - m_new), 0.0)
        alpha = jnp.exp2(m_old - m_new)
        m_ref[...] = m_new
        l_ref[...] = alpha * l_ref[...] + jnp.sum(pr, axis=-1, keepdims=True)
        acc_ref[...] = alpha * acc_ref[...] + values(pr.astype(BF16))

    span = SEL_PAGES * page_rows
    blk0 = j * (span // NSA_BLOCK)
    tile = blk0 // LANES
    lane = lax.broadcasted_iota(jnp.int32, (1, span), 1)
    scores = jnp.concatenate([_dot(qbd, pg[:kvw, :].astype(BF16)) for pg in pages], axis=1)

    def page_values(pr):
        out = jnp.zeros((rows, kvw), F32)
        for i, pg in enumerate(pages):
            out = out + _dot_nt(pr[:, i * page_rows:(i + 1) * page_rows], pg[kvw:, :].astype(BF16))
        return out

    attend(scores, page_values, j * span - past + lane, blk0 - tile * LANES + lane // NSA_BLOCK,
           sel_ref[tile], True)

    @pl.when(j == n_steps - 1)
    def _():
        lane1 = lax.broadcasted_iota(jnp.int32, (1, LANES), 1)
        newk = _pad_rows(new_ref[0, :, :kvw], LANES).astype(BF16)
        newv = _pad_rows(new_ref[0, :, kvw:], LANES).astype(BF16)
        nblk0 = past // NSA_BLOCK
        ntile = nblk0 // LANES
        attend(_dot_nt(qbd, newk), lambda pr: _dot(pr, newv), lane1,
               nblk0 - ntile * LANES + lane1 // NSA_BLOCK, sel_ref[ntile], lane1 < tn)
        o_s = _head_diag(acc_ref[...] / jnp.maximum(l_ref[...], 1e-30), NSA_KV_HEADS, rpk, HD)

        wb = win_ref.shape[2]
        wnk = _pad_rows(wnew_ref[0, :, :kvw], LANES).astype(BF16)
        wnv = _pad_rows(wnew_ref[0, :, kvw:], LANES).astype(BF16)
        sc = jnp.concatenate([_dot(qbd, win_ref[0, :kvw, :].astype(BF16)), _dot_nt(qbd, wnk)], axis=1)
        lane_w = lax.broadcasted_iota(jnp.int32, (1, wb + LANES), 1)
        krel = lane_w - wb
        dist = tok - krel
        valid = (dist >= 0) & (dist < NSA_WINDOW) & (krel + past >= 0) & (lane_w < wb + tn)
        pr = _softmax2(sc + slope * krel.astype(F32), valid).astype(BF16)
        o_w = _head_diag(_dot_nt(pr[:, :wb], win_ref[0, kvw:, :].astype(BF16)) + _dot(pr[:, wb:], wnv),
                         NSA_KV_HEADS, rpk, HD)

        gt = gate_ref[0]
        out = gt[:, 0:1] * oc_ref[...] + gt[:, 1:2] * o_s + gt[:, 2:3] * o_w
        for hd_i in range(NSA_HEADS):
            o_ref[0, :, hd_i * HD:(hd_i + 1) * HD] = out[hd_i * tn:(hd_i + 1) * tn]


def _nsa_sample(page_table, cache_t, qbd, q_rows, ckcv, new_sel, win_t, win_new, gates, nb):
    bd, pps = page_table.shape
    tn = new_sel.shape[1]
    rows = NSA_HEADS * tn
    kvw = NSA_KV_HEADS * HD
    n_steps = pps // SEL_PAGES
    per_b = lambda shape: pl.BlockSpec((1,) + shape, lambda b, j, pt: (b,) + (0,) * len(shape))
    in_specs = _page_specs(SEL_PAGES, 2 * kvw, 1, pps) + [
        per_b((rows, kvw)), per_b((rows, HD)), per_b((2 * NSA_KV_HEADS, CK_ROWS, HD)),
        per_b((tn, 2 * kvw)), per_b(win_t.shape[1:]), per_b((tn, 2 * kvw)), per_b((rows, LANES))]
    grid_spec = pltpu.PrefetchScalarGridSpec(
        num_scalar_prefetch=1, grid=(bd, n_steps), in_specs=in_specs,
        out_specs=per_b((tn, NQ_W)),
        scratch_shapes=[pltpu.VMEM((CK_ROWS // LANES, rows, LANES), F32), pltpu.VMEM((rows, HD), F32),
                        pltpu.VMEM((rows, 1), F32), pltpu.VMEM((rows, 1), F32),
                        pltpu.VMEM((rows, kvw), F32)])
    return pl.pallas_call(
        functools.partial(_nsa_sample_kernel, tn, nb, n_steps), grid_spec=grid_spec,
        out_shape=jax.ShapeDtypeStruct((bd, tn, NQ_W), F32),
        compiler_params=_params(("arbitrary", "arbitrary")), name="nsa_sample",
    )(page_table.reshape(-1), *([cache_t] * SEL_PAGES), qbd, q_rows, ckcv, new_sel, win_t, win_new, gates)


def _diff_sample_kernel(tn, lam_init, n_steps, pt_ref, *refs):
    pages = refs[:DIFF_PAGES]
    qbd_ref, new_ref, lqk_ref, gsub_ref, o_ref, m_ref, l_ref, acc_ref = refs[DIFF_PAGES:]
    j = pl.program_id(1)
    page_rows = pages[0].shape[0] // DIFF_ROWS
    past = n_steps * DIFF_PAGES * page_rows
    rph = 2 * tn
    rows = DIFF_HEADS * rph
    ridx = lax.broadcasted_iota(jnp.int32, (rows, 1), 0)
    tok = ridx % tn
    slope = _pow2_neg(2 * (ridx // rph + 1)) * LOG2E

    @pl.when(j == 0)
    def _():
        m_ref[...] = jnp.full(m_ref.shape, NEG, F32)
        l_ref[...] = jnp.zeros(l_ref.shape, F32)
        acc_ref[...] = jnp.zeros(acc_ref.shape, F32)

    def attend(key_sets, krel, extra_valid):
        sc = jnp.concatenate(
            [jnp.concatenate([_dot_nt(qbd_ref[0, hd_i], ks(hd_i, 0).astype(BF16)) for ks in key_sets], axis=1)
             for hd_i in range(DIFF_HEADS)], axis=0)
        valid = (krel <= tok) & extra_valid
        sc = jnp.where(valid, sc + slope * krel.astype(F32), NEG)
        m_old = m_ref[...]
        m_new = jnp.maximum(m_old, jnp.max(sc, axis=-1, keepdims=True))
        pr = jnp.where(valid, jnp.exp2(sc - m_new), 0.0)
        alpha = jnp.exp2(m_old - m_new)
        m_ref[...] = m_new
        l_ref[...] = alpha * l_ref[...] + jnp.sum(pr, axis=-1, keepdims=True)
        pr = pr.astype(BF16)
        n = sc.shape[1] // len(key_sets)
        pvs = []
        for hd_i in range(DIFF_HEADS):
            pv = jnp.zeros((rph, DIFF_VD), F32)
            for i, ks in enumerate(key_sets):
                pv = pv + _dot(pr[hd_i * rph:(hd_i + 1) * rph, i * n:(i + 1) * n], ks(hd_i, 1).astype(BF16))
            pvs.append(pv)
        acc_ref[...] = alpha * acc_ref[...] + jnp.concatenate(pvs, axis=0)

    span = DIFF_PAGES * page_rows
    krel = j * span - past + lax.broadcasted_iota(jnp.int32, (1, span), 1)
    attend([lambda hd_i, kind, pg=pg: pg[pl.ds(kind * DIFF_HEADS + hd_i, page_rows, stride=DIFF_ROWS), :]
            for pg in pages], krel, True)

    @pl.when(j == n_steps - 1)
    def _():
        lane = lax.broadcasted_iota(jnp.int32, (1, LANES), 1)
        new_rows = lambda hd_i, kind: _pad_rows(
            new_ref[0, pl.ds(kind * DIFF_HEADS + hd_i, tn, stride=DIFF_ROWS), :], LANES)
        attend([new_rows], lane, lane < tn)
        o = acc_ref[...] / jnp.maximum(l_ref[...], 1e-30)
        lam = _diff_lambda(lqk_ref, lam_init)
        for hd_i in range(DIFF_HEADS):
            o1 = o[hd_i * rph:hd_i * rph + tn]
            o2 = o[hd_i * rph + tn:(hd_i + 1) * rph]
            od = _rms(o1 - lam * o2, gsub_ref[...]) * (1.0 - lam_init)
            o_ref[0, :, hd_i * DIFF_VD:(hd_i + 1) * DIFF_VD] = od


def _diff_sample(page_table, cache_r, qbd, new_diff, lqk, gsub, lam_init):
    bd, pps = page_table.shape
    tn = new_diff.shape[1] // DIFF_ROWS
    rows = 2 * DIFF_HEADS * tn
    n_steps = pps // DIFF_PAGES
    per_b = lambda shape: pl.BlockSpec((1,) + shape, lambda b, j, pt: (b,) + (0,) * len(shape))
    in_specs = _page_specs(DIFF_PAGES, cache_r.shape[1], 0, pps) + [
        per_b(qbd.shape[1:]), per_b(new_diff.shape[1:]),
        pl.BlockSpec(lqk.shape, lambda b, j, pt: (0, 0)),
        pl.BlockSpec(gsub.shape, lambda b, j, pt: (0, 0))]
    grid_spec = pltpu.PrefetchScalarGridSpec(
        num_scalar_prefetch=1, grid=(bd, n_steps), in_specs=in_specs,
        out_specs=per_b((tn, DV_W)),
        scratch_shapes=[pltpu.VMEM((rows, 1), F32), pltpu.VMEM((rows, 1), F32),
                        pltpu.VMEM((rows, DIFF_VD), F32)])
    return pl.pallas_call(
        functools.partial(_diff_sample_kernel, tn, lam_init, n_steps), grid_spec=grid_spec,
        out_shape=jax.ShapeDtypeStruct((bd, tn, DV_W), F32),
        compiler_params=_params(("arbitrary", "arbitrary")), name="diff_sample",
    )(page_table.reshape(-1), *([cache_r] * DIFF_PAGES), qbd, new_diff, lqk, gsub)


def _block_diag2(a):
    z = jnp.zeros_like(a)
    return jnp.concatenate([jnp.concatenate([a, z], axis=-1), jnp.concatenate([z, a], axis=-1)], axis=-2)


def _layer_weights(l, g_ffn1_pre, ffn1_w1, ffn1_w3, ffn1_w2, g_ffn1_post, g_mix_pre, w_in, cmp_w1,
                   cmp_b1, cmp_w2, lambda_qk, g_diff_subln, w_up_nsa, w_up_diff, w_out, g_mix_post,
                   g_ffn2_pre, ffn2_w1, ffn2_w3, ffn2_w2, g_ffn2_post):
    assert NSA_KV_HEADS == 2 and 2 * NSA_BLOCK == LANES
    row = lambda g: g[l][None, :]
    b16 = lambda w: w[l].astype(BF16)
    wi = w_in[l]
    o_ng = NQ_W + NKV_W
    o_dq = o_ng + NG_W
    o_dv = o_dq + 2 * DQ_W
    o_mg = o_dv + DV_W
    ng_cols = jnp.concatenate([wi[:, o_ng:o_dq], jnp.zeros((wi.shape[0], LANES - NG_W), wi.dtype)], axis=1)
    wp = jnp.concatenate([wi[:, :o_ng], wi[:, o_dq:o_mg], ng_cols], axis=1).astype(BF16)
    wt = jnp.concatenate([wi[:, NQ_W:o_ng], wi[:, o_dv:o_mg], ng_cols], axis=1).T.astype(BF16)
    w1c = cmp_w1[l]
    return dict(
        ffn1=(row(g_ffn1_pre), b16(ffn1_w1), b16(ffn1_w3), b16(ffn1_w2), row(g_ffn1_post)),
        ffn2=(row(g_ffn2_pre), b16(ffn2_w1), b16(ffn2_w3), b16(ffn2_w2), row(g_ffn2_post)),
        gmix=row(g_mix_pre), wp=wp, wt=wt, wmg=wi[:, o_mg:].astype(BF16),
        w1r=w1c.reshape(2, NSA_BLOCK * HD, HD).astype(BF16), b1=cmp_b1[l][:, None, :],
        w2c=cmp_w2[l].astype(BF16),
        wd=_block_diag2(w1c.transpose(0, 2, 1, 3)).reshape(2, HD * LANES, LANES).astype(BF16),
        bd2=_block_diag2(cmp_w2[l]).astype(BF16),
        b1t=jnp.tile(cmp_b1[l], (1, 2))[:, None, :],
        lqk=lambda_qk[l], gsub=row(g_diff_subln), wun=b16(w_up_nsa), wud=b16(w_up_diff),
        wout=b16(w_out), gmixpost=row(g_mix_post))


def _prompt_layer(x, W, lam_init):
    bsz, t, d = x.shape
    n = bsz * t
    (x1, q_sep, dq, diff_new, cmp_sep, nsa_new_t, win_new_t, k_rows, v_t, dk, dv_t, ng_t) = _ffn_proj(
        x.reshape(n, d), *W["ffn1"], W["gmix"], W["wp"], W["wt"], tm=512, seq_len=t)
    nb = t // NSA_BLOCK
    xc = cmp_sep.reshape(2 * NSA_KV_HEADS, n // NSA_BLOCK, NSA_BLOCK * HD)
    ckcv = _compress_prompt(xc, W["w1r"], W["b1"], W["w2c"])
    ck = ckcv.reshape(2 * NSA_KV_HEADS, bsz, nb, HD).transpose(1, 0, 2, 3)
    o_nsa = _nsa_prompt(q_sep, ck, ck.transpose(0, 1, 3, 2), k_rows, v_t, ng_t, bsz, t)
    o_diff = _diff_prompt(dq, dk, dv_t, W["lqk"], W["gsub"], lam_init, bsz, t)
    y = _merge_ffn(x1, o_nsa, o_diff, W["gmix"], W["wmg"], W["wun"], W["wud"], W["wout"],
                   W["gmixpost"], *W["ffn2"], tm=512)
    w_rows = min(NSA_WINDOW, t)
    new_nsa = nsa_new_t.reshape(bsz, 4, NSA_KV_HEADS, HD, t).transpose(0, 4, 1, 2, 3)
    new_win = win_new_t[:, :, t - w_rows:].reshape(bsz, 2, NSA_KV_HEADS, HD, w_rows).transpose(0, 4, 1, 2, 3)
    new_diff = diff_new.reshape(bsz, t, 2, DIFF_HEADS, 2 * HD)
    return y.reshape(bsz, t, d), (new_nsa, new_diff, new_win)


def _sample_layer(x, c_nsa, c_diff, win, page_table, W, lam_init):
    bd, tn, d = x.shape
    n = bd * tn
    n_pool, page_rows = c_nsa.shape[:2]
    pps = page_table.shape[1]
    past = pps * page_rows
    nb = -(-(past + tn) // NSA_BLOCK)
    assert page_rows == LANES and nb <= CK_ROWS
    (x1, q_sep, dq, diff_new, nsa_new, win_new, ng) = _ffn_proj(
        x.reshape(n, d), *W["ffn1"], W["gmix"], W["wp"], W["wt"], tm=n)
    nsa_t = c_nsa.transpose(0, 2, 3, 4, 1).reshape(n_pool, NSA_NEW_W, page_rows)
    diff_r = c_diff.reshape(n_pool, page_rows * DIFF_ROWS, LANES)
    win_t = win.transpose(0, 2, 3, 4, 1).reshape(bd, WIN_W, win.shape[1])
    nsa_new3 = nsa_new.reshape(bd, tn, NSA_NEW_W)

    new_cmp_t = jnp.pad(nsa_new3[:, :, :CMP_W].transpose(0, 2, 1), ((0, 0), (0, 0), (0, LANES - tn)))
    ck_pages = _compress_sample(page_table, nsa_t, new_cmp_t, W["wd"], W["b1t"], W["bd2"])
    slots = ck_pages.shape[2]
    ckcv = ck_pages.reshape(bd, 2 * NSA_KV_HEADS, slots * 2, HD)[:, :, :nb]
    ckcv = jnp.pad(ckcv, ((0, 0), (0, 0), (0, CK_ROWS - nb), (0, 0)))

    q_rows = q_sep.reshape(NSA_HEADS, bd, tn, HD).transpose(1, 0, 2, 3).reshape(bd, NSA_HEADS * tn, HD)
    kv_of_row = jnp.arange(NSA_HEADS * tn) // (NSA_GROUP * tn)
    qbd_n = jnp.concatenate(
        [jnp.where((kv_of_row == kvh)[None, :, None], q_rows, 0) for kvh in range(NSA_KV_HEADS)], axis=-1)
    gates = ng[:, :NG_W].reshape(bd, tn, NSA_HEADS, 3).transpose(0, 2, 1, 3).reshape(bd, NSA_HEADS * tn, 3)
    gates = jnp.pad(gates, ((0, 0), (0, 0), (0, LANES - 3)))
    o_nsa = _nsa_sample(page_table, nsa_t, qbd_n, q_rows, ckcv, nsa_new3[:, :, CMP_W:], win_t,
                        win_new.reshape(bd, tn, WIN_W), gates, nb)

    dq5 = dq.reshape(DIFF_HEADS, 2, bd, tn, HD).transpose(2, 0, 1, 3, 4)
    z = jnp.zeros_like(dq5[:, :, 0])
    qbd_d = jnp.concatenate([jnp.concatenate([dq5[:, :, 0], z], axis=-1),
                             jnp.concatenate([z, dq5[:, :, 1]], axis=-1)], axis=2)
    o_diff = _diff_sample(page_table, diff_r, qbd_d, diff_new.reshape(bd, tn * DIFF_ROWS, LANES),
                          W["lqk"], W["gsub"], lam_init)
    y = _merge_ffn(x1, o_nsa.reshape(n, NQ_W), o_diff.reshape(n, DV_W), W["gmix"], W["wmg"],
                   W["wun"], W["wud"], W["wout"], W["gmixpost"], *W["ffn2"], tm=n)
    new_nsa = nsa_new.reshape(bd, tn, 4, NSA_KV_HEADS, HD)
    new_diff = diff_new.reshape(bd, tn, 2, DIFF_HEADS, 2 * HD)
    new_win = jnp.concatenate([win, win_new.reshape(bd, tn, 2, NSA_KV_HEADS, HD)], axis=1)[:, tn:]
    return y.reshape(bd, tn, d), (new_nsa, new_diff, new_win)


def kernel(x_prompt, x_sample, cache_nsa_kv, cache_diff_kv, state_win_kv, page_table, g_ffn1_pre, ffn1_w1, ffn1_w3, ffn1_w2, g_ffn1_post, g_mix_pre, w_in, cmp_w1, cmp_b1, cmp_w2, lambda_qk, g_diff_subln, w_up_nsa, w_up_diff, w_out, g_mix_post, g_ffn2_pre, ffn2_w1, ffn2_w3, ffn2_w2, g_ffn2_post):
    depth = w_in.shape[0]
    y_p, y_s = x_prompt, x_sample
    outs = [[] for _ in range(6)]
    for l in range(depth):
        lam_init = 0.8 - 0.6 * math.exp(-0.3 * l)
        W = _layer_weights(l, g_ffn1_pre, ffn1_w1, ffn1_w3, ffn1_w2, g_ffn1_post, g_mix_pre, w_in,
                           cmp_w1, cmp_b1, cmp_w2, lambda_qk, g_diff_subln, w_up_nsa, w_up_diff,
                           w_out, g_mix_post, g_ffn2_pre, ffn2_w1, ffn2_w3, ffn2_w2, g_ffn2_post)
        y_p, (a_p, b_p, c_p) = _prompt_layer(y_p, W, lam_init)
        y_s, (a_s, b_s, c_s) = _sample_layer(y_s, cache_nsa_kv[l], cache_diff_kv[l], state_win_kv[l],
                                             page_table, W, lam_init)
        for lst, val in zip(outs, (a_p, a_s, b_p, b_s, c_p, c_s)):
            lst.append(val)
    return (y_p, y_s) + tuple(jnp.stack(o) for o in outs)
```
